```python
import functools
import jax
import jax.numpy as jnp
from jax import lax
import numpy as np

D_MODEL = 1024
BATCH = 8
SEQ = 4096
DEPTH = 1
DEC_BATCH = 128
DEC_SEQ = 4
PAST_LEN = 8192
PAGE_SIZE = 128

N_HEADS_A = 8
N_KV_A = 2
HEAD_DIM_A = 64
W_A = N_HEADS_A * HEAD_DIM_A
N_IDX_HEADS = 4
IDX_DIM = 64
TOPK_MAX = 256
Q_BLOCK = 128
ROPE_THETA = 500000.0
ROT_A = HEAD_DIM_A // 4
ROT_IDX = IDX_DIM // 4
N_HEADS_B = 8
HEAD_DIM_B = 64
W_B = N_HEADS_B * HEAD_DIM_B
D_DECAY_LORA = 64
D_AAA_LORA = 64
D_GATE_LORA = 128
RWKV_PROJ = 3 * W_B + D_DECAY_LORA + D_AAA_LORA + D_GATE_LORA
GN_EPS = 64e-5
N_MEM = 256
N_HEADS_C = 4
HEAD_DIM_C = 128
W_C = N_HEADS_C * HEAD_DIM_C
N_BRANCH = 3
D_FF = 4 * D_MODEL
RMS_EPS = 1e-6
IN_SIZES = (W_A, N_KV_A * HEAD_DIM_A, N_KV_A * HEAD_DIM_A, N_IDX_HEADS * IDX_DIM, IDX_DIM,
            N_IDX_HEADS, RWKV_PROJ, W_C, N_BRANCH * D_MODEL)
IN_WIDTH = (W_A + 2 * N_KV_A * HEAD_DIM_A + N_IDX_HEADS * IDX_DIM + IDX_DIM + N_IDX_HEADS
            + RWKV_PROJ + W_C + N_BRANCH * D_MODEL)
RWKV_SIZES = (W_B, W_B, W_B, D_DECAY_LORA, D_AAA_LORA, D_GATE_LORA)

kernel_name = "hybrid_dsa_rwkv7_memxattn_step"


def split_last(p, sizes):
    offs = []
    acc = 0
    for s in sizes[:-1]:
        acc += s
        offs.append(acc)
    return jnp.split(p, offs, axis=-1)


def rmsnorm(x, g):
    xf = x.astype(jnp.float32)
    y = xf * lax.rsqrt(jnp.mean(xf * xf, axis=-1, keepdims=True) + RMS_EPS)
    return (y * g.astype(jnp.float32)).astype(x.dtype)


def rope_partial(x, pos, rot_dim):
    half = rot_dim // 2
    inv_freq = ROPE_THETA ** (-jnp.arange(half, dtype=jnp.float32) / half)
    ang = pos.astype(jnp.float32)[:, None] * inv_freq[None, :]
    cos = jnp.cos(ang)[:, None, :]
    sin = jnp.sin(ang)[:, None, :]
    xf = x.astype(jnp.float32)
    x1, x2, rest = xf[..., :half], xf[..., half:rot_dim], xf[..., rot_dim:]
    out = jnp.concatenate([x1 * cos - x2 * sin, x2 * cos + x1 * sin, rest], axis=-1)
    return out.astype(x.dtype)


def index_topk(iq, iw, ik, q_pos, topk):
    s = jax.nn.relu(jnp.einsum('bqhd,bld->bqhl', iq, ik))
    score = jnp.einsum('bqhl,bqh->bql', s, iw).astype(jnp.float32)
    n_keys = ik.shape[1]
    causal = jnp.arange(n_keys)[None, None, :] <= q_pos[None, :, None]
    score = jnp.where(causal, score, -jnp.inf)
    _, idx = lax.top_k(score, topk)
    valid = idx <= q_pos[None, :, None]
    return idx, valid


def sparse_attend(q, k_sel, v_sel, valid):
    B, Tq = q.shape[:2]
    qg = q.reshape(B, Tq, N_KV_A, N_HEADS_A // N_KV_A, HEAD_DIM_A)
    s = jnp.einsum('bqgrd,bqkgd->bqgrk', qg, k_sel).astype(jnp.float32) * (HEAD_DIM_A ** -0.5)
    s = jnp.where(valid[:, :, None, None, :], s, -jnp.inf)
    p = jax.nn.softmax(s, axis=-1).astype(v_sel.dtype)
    o = jnp.einsum('bqgrk,bqkgd->bqgrd', p, v_sel)
    return o.reshape(B, Tq, W_A)


def dsa_prompt(q, k, v, iq, ik, iw):
    B, T = q.shape[:2]
    topk = min(TOPK_MAX, T // 4)
    nb = T // Q_BLOCK
    bsel = jnp.arange(B)[:, None, None]

    def blocks(a):
        return a.reshape((B, nb, Q_BLOCK) + a.shape[2:]).swapaxes(0, 1)

    def one_block(args):
        qb, iqb, iwb, bi = args
        q_pos = bi * Q_BLOCK + jnp.arange(Q_BLOCK)
        idx, valid = index_topk(iqb, iwb, ik, q_pos, topk)
        return sparse_attend(qb, k[bsel, idx], v[bsel, idx], valid)

    o = lax.map(one_block, (blocks(q), blocks(iq), blocks(iw), jnp.arange(nb)))
    return o.swapaxes(0, 1).reshape(B, T, W_A)


def dsa_sample(q, k_new, v_new, iq, ik_new, iw, cache_k, cache_v, cache_idx_k, page_table):
    DB, Tn = q.shape[:2]
    n_keys = PAST_LEN + Tn
    topk = min(TOPK_MAX, n_keys // 4)
    ik_past = cache_idx_k[page_table].reshape(DB, PAST_LEN, IDX_DIM)
    ik_all = jnp.concatenate([ik_past, ik_new.astype(ik_past.dtype)], axis=1)
    q_pos = PAST_LEN + jnp.arange(Tn)
    idx, valid = index_topk(iq, iw, ik_all, q_pos, topk)
    bsel = jnp.arange(DB)[:, None, None]
    in_cache = (idx < PAST_LEN)[..., None, None]
    pc = jnp.minimum(idx, PAST_LEN - 1)
    phys = page_table[bsel, pc // PAGE_SIZE]
    off = pc % PAGE_SIZE
    pn = jnp.clip(idx - PAST_LEN, 0, Tn - 1)
    k_sel = jnp.where(in_cache, cache_k[phys, off], k_new[bsel, pn].astype(cache_k.dtype))
    v_sel = jnp.where(in_cache, cache_v[phys, off], v_new[bsel, pn].astype(cache_v.dtype))
    return sparse_attend(q, k_sel.astype(q.dtype), v_sel.astype(q.dtype), valid)


def rwkv7_time_mix(p_b, shift_prev, wkv_prev, mu, w0, w_up, a0, a_up, g_up, k_k, k_a, r_k, ln_g, ln_b):
    B, T, _ = p_b.shape
    prev = jnp.concatenate([shift_prev[:, None, :].astype(p_b.dtype), p_b[:, :-1]], axis=1)
    ps = p_b + (prev - p_b) * mu
    r, k, v, wl, al, gl = split_last(ps, RWKV_SIZES)
    w_log = -jax.nn.softplus(-(w0 + jnp.tanh(wl) @ w_up)) - 0.5
    decay = jnp.exp(-jnp.exp(w_log.astype(jnp.float32)))
    a = jax.nn.sigmoid(a0 + al @ a_up)
    g = jax.nn.sigmoid(gl) @ g_up

    def heads(t):
        return t.reshape(B, T, N_HEADS_B, HEAD_DIM_B).astype(jnp.float32)

    r_h, v_h, a_h, w_h = heads(r), heads(v), heads(a), heads(decay)
    kk = heads(k * k_k)
    kk = kk / jnp.maximum(jnp.sqrt(jnp.sum(kk * kk, axis=-1, keepdims=True)), 1e-12)
    k_h = heads(k) * (1.0 + (a_h - 1.0) * k_a.reshape(N_HEADS_B, HEAD_DIM_B).astype(jnp.float32))

    def step(S, inp):
        r_t, w_t, k_t, v_t, kk_t, a_t = inp
        sa = jnp.einsum('bhvk,bhk->bhv', S, -kk_t)
        S = (S * w_t[:, :, None, :] + sa[..., None] * (kk_t * a_t)[:, :, None, :]
             + v_t[..., None] * k_t[:, :, None, :])
        return S, jnp.einsum('bhvk,bhk->bhv', S, r_t)

    xs = tuple(t.swapaxes(0, 1) for t in (r_h, w_h, k_h, v_h, kk, a_h))
    S_T, y = lax.scan(step, wkv_prev.astype(jnp.float32), xs)
    y = y.swapaxes(0, 1)
    mean = jnp.mean(y, axis=-1, keepdims=True)
    var = jnp.mean(jnp.square(y - mean), axis=-1, keepdims=True)
    yn = ((y - mean) * lax.rsqrt(var + GN_EPS)).reshape(B, T, W_B)
    yn = yn * ln_g.astype(jnp.float32) + ln_b.astype(jnp.float32)
    bonus = jnp.sum(r_h * k_h * r_k.astype(jnp.float32), axis=-1, keepdims=True) * v_h
    out = (yn + bonus.reshape(B, T, W_B)) * g.astype(jnp.float32)
    return out.astype(p_b.dtype), S_T, p_b[:, -1]


def memory_kv(mem, norm_mem_g, w_mem_kv):
    B = mem.shape[0]
    mk, mv = jnp.split(rmsnorm(mem, norm_mem_g) @ w_mem_kv, 2, axis=-1)
    return (mk.reshape(B, N_MEM, N_HEADS_C, HEAD_DIM_C), mv.reshape(B, N_MEM, N_HEADS_C, HEAD_DIM_C))


def cross_attend(qc, mk, mv):
    B, T = qc.shape[:2]
    s = jnp.einsum('bthd,bmhd->bhtm', qc, mk.astype(qc.dtype)).astype(jnp.float32) * (HEAD_DIM_C ** -0.5)
    p = jax.nn.softmax(s, axis=-1).astype(qc.dtype)
    o = jnp.einsum('bhtm,bmhd->bthd', p, mv.astype(qc.dtype))
    return o.reshape(B, T, W_C)


def token_mixing(x, pos, dsa_fn, mem_k, mem_v, shift_prev, wkv_prev, norm_mix_g, w_in, b_gate,
                 rwkv_params, w_branch_a, w_branch_b, w_branch_c, w_out):
    B, T, _ = x.shape
    p = rmsnorm(x, norm_mix_g) @ w_in
    a_q, a_k, a_v, i_q, i_k, i_w, p_b, c_q, gates = split_last(p, IN_SIZES)
    a_q = rope_partial(a_q.reshape(B, T, N_HEADS_A, HEAD_DIM_A), pos, ROT_A)
    a_k = rope_partial(a_k.reshape(B, T, N_KV_A, HEAD_DIM_A), pos, ROT_A)
    a_v = a_v.reshape(B, T, N_KV_A, HEAD_DIM_A)
    i_q = rope_partial(i_q.reshape(B, T, N_IDX_HEADS, IDX_DIM), pos, ROT_IDX)
    i_k = rope_partial(i_k[:, :, None, :], pos, ROT_IDX)[:, :, 0, :]
    i_w = i_w * ((N_IDX_HEADS * IDX_DIM) ** -0.5)
    o_a = dsa_fn(a_q, a_k, a_v, i_q, i_k, i_w)
    o_b, wkv_new, shift_new = rwkv7_time_mix(p_b, shift_prev, wkv_prev, *rwkv_params)
    o_c = cross_attend(c_q.reshape(B, T, N_HEADS_C, HEAD_DIM_C), mem_k, mem_v)
    g = jax.nn.sigmoid(gates + b_gate).reshape(B, T, N_BRANCH, D_MODEL)
    merged = (g[:, :, 0] * (o_a @ w_branch_a) + g[:, :, 1] * (o_b @ w_branch_b)
              + g[:, :, 2] * (o_c @ w_branch_c))
    return x + merged @ w_out, a_k, a_v, i_k, wkv_new, shift_new


def sq_relu_mlp(x, norm_g, w_up, w_down):
    h = rmsnorm(x, norm_g) @ w_up
    return x + jnp.square(jax.nn.relu(h)) @ w_down


def setup_inputs(seed: int = 0) -> dict:
    key = jax.random.key(seed)
    ks = iter(jax.random.split(key, 48))

    def nrm(shape, scale):
        return jax.random.normal(next(ks), shape, jnp.float32) * scale

    def unif(shape, lo, hi):
        return jax.random.uniform(next(ks), shape, jnp.float32, lo, hi)

    n_pages = PAST_LEN // PAGE_SIZE
    n_phys = (DEC_BATCH * n_pages * 5) // 4
    page_table = jax.random.permutation(next(ks), n_phys)[:DEC_BATCH * n_pages]
    page_table = page_table.reshape(DEC_BATCH, n_pages).astype(jnp.int32)
    return {
        "x_prompt": nrm((BATCH, SEQ, D_MODEL), 1.0),
        "x_sample": nrm((DEC_BATCH, DEC_SEQ, D_MODEL), 1.0),
        "mem_prompt": nrm((BATCH, N_MEM, D_MODEL), 1.0),
        "cache_k": nrm((n_phys, PAGE_SIZE, N_KV_A, HEAD_DIM_A), 1.0),
        "cache_v": nrm((n_phys, PAGE_SIZE, N_KV_A, HEAD_DIM_A), 1.0),
        "cache_idx_k": nrm((n_phys, PAGE_SIZE, IDX_DIM), 1.0),
        "cache_mem_k": nrm((DEC_BATCH, N_MEM, N_HEADS_C, HEAD_DIM_C), 1.0),
        "cache_mem_v": nrm((DEC_BATCH, N_MEM, N_HEADS_C, HEAD_DIM_C), 1.0),
        "state_wkv": nrm((DEC_BATCH, N_HEADS_B, HEAD_DIM_B, HEAD_DIM_B), 0.3),
        "state_shift": nrm((DEC_BATCH, RWKV_PROJ), 1.0),
        "page_table": page_table,
        "norm_mix_g": 1.0 + nrm((D_MODEL,), 0.05),
        "norm_mem_g": 1.0 + nrm((D_MODEL,), 0.05),
        "norm_mlp_g": 1.0 + nrm((D_MODEL,), 0.05),
        "norm_final_g": 1.0 + nrm((D_MODEL,), 0.05),
        "w_in": nrm((D_MODEL, IN_WIDTH), D_MODEL ** -0.5),
        "b_gate": nrm((N_BRANCH * D_MODEL,), 0.02),
        "w_mem_kv": nrm((D_MODEL, 2 * W_C), D_MODEL ** -0.5),
        "rwkv_mu": unif((RWKV_PROJ,), 0.0, 1.0),
        "rwkv_w0": unif((W_B,), -4.0, 1.0),
        "rwkv_w_up": nrm((D_DECAY_LORA, W_B), 0.1 * D_DECAY_LORA ** -0.5),
        "rwkv_a0": nrm((W_B,), 0.1),
        "rwkv_a_up": nrm((D_AAA_LORA, W_B), 0.5 * D_AAA_LORA ** -0.5),
        "rwkv_g_up": nrm((D_GATE_LORA, W_B), D_GATE_LORA ** -0.5),
        "rwkv_k_k": 0.85 + nrm((W_B,), 0.05),
        "rwkv_k_a": 1.0 + nrm((W_B,), 0.05),
        "rwkv_r_k": nrm((N_HEADS_B, HEAD_DIM_B), 0.1),
        "rwkv_ln_g": 1.0 + nrm((W_B,), 0.05),
        "rwkv_ln_b": nrm((W_B,), 0.02),
        "w_branch_a": nrm((W_A, D_MODEL), W_A ** -0.5),
        "w_branch_b": nrm((W_B, D_MODEL), W_B ** -0.5),
        "w_branch_c": nrm((W_C, D_MODEL), W_C ** -0.5),
        "w_out": nrm((D_MODEL, D_MODEL), D_MODEL ** -0.5),
        "w_mlp_up": nrm((D_MODEL, D_FF), D_MODEL ** -0.5),
        "w_mlp_down": nrm((D_FF, D_MODEL), D_FF ** -0.5),
    }


def reference(x_prompt, x_sample, mem_prompt, cache_k, cache_v, cache_idx_k, cache_mem_k, cache_mem_v,
              state_wkv, state_shift, page_table, norm_mix_g, norm_mem_g, norm_mlp_g, norm_final_g,
              w_in, b_gate, w_mem_kv, rwkv_mu, rwkv_w0, rwkv_w_up, rwkv_a0, rwkv_a_up, rwkv_g_up,
              rwkv_k_k, rwkv_k_a, rwkv_r_k, rwkv_ln_g, rwkv_ln_b, w_branch_a, w_branch_b, w_branch_c,
              w_out, w_mlp_up, w_mlp_down):
    rwkv_params = (rwkv_mu, rwkv_w0, rwkv_w_up, rwkv_a0, rwkv_a_up, rwkv_g_up, rwkv_k_k, rwkv_k_a,
                   rwkv_r_k, rwkv_ln_g, rwkv_ln_b)

    B, T, _ = x_prompt.shape
    pos_p = jnp.arange(T)
    shift0 = jnp.zeros((B, RWKV_PROJ), x_prompt.dtype)
    wkv0 = jnp.zeros((B, N_HEADS_B, HEAD_DIM_B, HEAD_DIM_B), jnp.float32)
    y_p = x_prompt
    for _layer in range(DEPTH):
        mem_k_p, mem_v_p = memory_kv(mem_prompt, norm_mem_g, w_mem_kv)
        h_p, k_p, v_p, ik_p, wkv_p, sh_p = token_mixing(
            y_p, pos_p, dsa_prompt, mem_k_p, mem_v_p, shift0, wkv0, norm_mix_g, w_in, b_gate,
            rwkv_params, w_branch_a, w_branch_b, w_branch_c, w_out)
        y_p = sq_relu_mlp(h_p, norm_mlp_g, w_mlp_up, w_mlp_down)
    y_prompt = rmsnorm(y_p, norm_final_g)

    pos_s = PAST_LEN + jnp.arange(x_sample.shape[1])
    dsa_s = functools.partial(dsa_sample, cache_k=cache_k, cache_v=cache_v,
                              cache_idx_k=cache_idx_k, page_table=page_table)
    y_s = x_sample
    for _layer in range(DEPTH):
        h_s, k_s, v_s, ik_s, wkv_s, sh_s = token_mixing(
            y_s, pos_s, dsa_s, cache_mem_k, cache_mem_v, state_shift, state_wkv, norm_mix_g, w_in,
            b_gate, rwkv_params, w_branch_a, w_branch_b, w_branch_c, w_out)
        y_s = sq_relu_mlp(h_s, norm_mlp_g, w_mlp_up, w_mlp_down)
    y_sample = rmsnorm(y_s, norm_final_g)

    return (y_prompt, y_sample, k_p, v_p, ik_p, mem_k_p, mem_v_p, wkv_p, sh_p,
            k_s, v_s, ik_s, wkv_s, sh_s)
```

```python
import functools
import math

import jax
import jax.numpy as jnp
from jax import lax
from jax.experimental import pallas as pl
from jax.experimental.pallas import tpu as pltpu

F32 = jnp.float32
BF16 = jnp.bfloat16

D_MODEL = 1024
N_HEADS_A, N_KV_A, HEAD_DIM_A = 8, 2, 64
W_A = N_HEADS_A * HEAD_DIM_A
W_KV = N_KV_A * HEAD_DIM_A
N_IDX_HEADS, IDX_DIM = 4, 64
W_IQ = N_IDX_HEADS * IDX_DIM
TOPK_MAX = 256
ROPE_THETA = 500000.0
ROT_DIM = 16
N_HEADS_B, HEAD_DIM_B = 8, 64
W_B = N_HEADS_B * HEAD_DIM_B
D_DECAY_LORA, D_AAA_LORA, D_GATE_LORA = 64, 64, 128
RWKV_PROJ = 3 * W_B + D_DECAY_LORA + D_AAA_LORA + D_GATE_LORA
GN_EPS = 64e-5
N_HEADS_C, HEAD_DIM_C = 4, 128
W_C = N_HEADS_C * HEAD_DIM_C
N_BRANCH = 3
D_FF = 4 * D_MODEL
RMS_EPS = 1e-6

LANES = 128
SUBLANES = 8
VMEM_LIMIT_BYTES = 56 * 1024 * 1024

INT_MIN = -(2 ** 31)

_W_IK_PAD = LANES - IDX_DIM - N_IDX_HEADS
SEG_QA = (0, W_A)
SEG_KA = (W_A, W_KV)
SEG_VA = (W_A + W_KV, W_KV)
SEG_IQ = (W_A + 2 * W_KV, W_IQ)
SEG_IKW = (SEG_IQ[0] + W_IQ, LANES)
SEG_PB = (SEG_IKW[0] + LANES, RWKV_PROJ)
SEG_CQ = (SEG_PB[0] + RWKV_PROJ, W_C)
SEG_G = (SEG_CQ[0] + W_C, N_BRANCH * D_MODEL)
W_PACKED = SEG_G[0] + SEG_G[1]


def _rmsnorm(x, g):
    ms = jnp.mean(x * x, axis=-1, keepdims=True)
    return x * lax.rsqrt(ms + RMS_EPS) * g


def _rope(y, cos, s_lo, s_hi):
    return y * cos + pltpu.roll(y, LANES - ROT_DIM // 2, axis=1) * s_lo + pltpu.roll(y, ROT_DIM // 2, axis=1) * s_hi


def _inproj_kernel(x_ref, g_ref, w_ref, bg_ref, cos_ref, slo_ref, shi_ref,
                   qa_ref, ka_ref, va_ref, iq_ref, ikw_ref, pb_ref, cq_ref, gs_ref):
    xn = _rmsnorm(x_ref[...], g_ref[...]).astype(BF16)
    cos, s_lo, s_hi = cos_ref[...], slo_ref[...], shi_ref[...]

    def proj(off, width):
        return jnp.dot(xn, w_ref[:, off:off + width], preferred_element_type=F32)

    def roped(seg, out_ref):
        for c in range(0, seg[1], LANES):
            out_ref[:, c:c + LANES] = _rope(proj(seg[0] + c, LANES), cos, s_lo, s_hi)

    roped(SEG_QA, qa_ref)
    roped(SEG_KA, ka_ref)
    va_ref[...] = proj(*SEG_VA)
    roped(SEG_IQ, iq_ref)
    y = proj(*SEG_IKW)
    lane = lax.broadcasted_iota(jnp.int32, y.shape, 1)
    ikw_ref[...] = jnp.where(lane < IDX_DIM, _rope(y, cos, s_lo, s_hi), y * (W_IQ ** -0.5))
    for c in range(0, SEG_PB[1], 256):
        pb_ref[:, c:c + 256] = proj(SEG_PB[0] + c, 256)
    cq_ref[...] = proj(*SEG_CQ)
    for c in range(0, SEG_G[1], 512):
        gs_ref[:, c:c + 512] = jax.nn.sigmoid(proj(SEG_G[0] + c, 512) + bg_ref[:, c:c + 512])


def _rope_tables(pos, rows):
    half = ROT_DIM // 2
    inv_freq = ROPE_THETA ** (-jnp.arange(half, dtype=F32) / half)
    ang = pos.astype(F32)[:, None] * inv_freq[None, :]
    cos, sin = jnp.cos(ang), jnp.sin(ang)
    t = pos.shape[0]
    ones = jnp.ones((t, HEAD_DIM_A - ROT_DIM), F32)
    zeros_h = jnp.zeros((t, half), F32)
    zeros_r = jnp.zeros((t, HEAD_DIM_A - ROT_DIM), F32)
    c64 = jnp.concatenate([cos, cos, ones], axis=1)
    lo64 = jnp.concatenate([-sin, zeros_h, zeros_r], axis=1)
    hi64 = jnp.concatenate([zeros_h, sin, zeros_r], axis=1)
    reps = rows // t
    return tuple(jnp.tile(jnp.concatenate([a, a], axis=1), (reps, 1)) for a in (c64, lo64, hi64))


def _inproj(x2, pos, norm_g, w_packed, b_gate, tm):
    n = x2.shape[0]
    t = pos.shape[0]
    rows = max(t, tm)
    cos, s_lo, s_hi = _rope_tables(pos, rows)
    nt = rows // tm
    row = lambda i: (i, 0)
    const = lambda i: (0, 0)
    tab = pl.BlockSpec((tm, LANES), lambda i: (i % nt, 0))
    widths = [SEG_QA[1], SEG_KA[1], SEG_VA[1], SEG_IQ[1], SEG_IKW[1], SEG_PB[1], SEG_CQ[1], SEG_G[1]]
    return pl.pallas_call(
        _inproj_kernel,
        grid=(n // tm,),
        in_specs=[
            pl.BlockSpec((tm, D_MODEL), row),
            pl.BlockSpec((1, D_MODEL), const),
            pl.BlockSpec((D_MODEL, W_PACKED), const, pipeline_mode=pl.Buffered(1)),
            pl.BlockSpec((1, SEG_G[1]), const),
            tab, tab, tab,
        ],
        out_specs=[pl.BlockSpec((tm, w), row) for w in widths],
        out_shape=[jax.ShapeDtypeStruct((n, w), F32) for w in widths],
        compiler_params=pltpu.CompilerParams(
            dimension_semantics=("arbitrary",), vmem_limit_bytes=VMEM_LIMIT_BYTES),
        name="inproj",
    )(x2, norm_g.reshape(1, D_MODEL), w_packed, b_gate.reshape(1, -1), cos, s_lo, s_hi)


def _memkv_kernel(x_ref, g_ref, w_ref, k_ref, v_ref):
    xn = _rmsnorm(x_ref[...], g_ref[...]).astype(BF16)
    k_ref[...] = jnp.dot(xn, w_ref[:, :W_C], preferred_element_type=F32)
    v_ref[...] = jnp.dot(xn, w_ref[:, W_C:], preferred_element_type=F32)


def _memkv(mem2, norm_g, w_bf16, tm):
    n = mem2.shape[0]
    row = lambda i: (i, 0)
    const = lambda i: (0, 0)
    return pl.pallas_call(
        _memkv_kernel,
        grid=(n // tm,),
        in_specs=[pl.BlockSpec((tm, D_MODEL), row), pl.BlockSpec((1, D_MODEL), const),
                  pl.BlockSpec((D_MODEL, 2 * W_C), const)],
        out_specs=[pl.BlockSpec((tm, W_C), row)] * 2,
        out_shape=[jax.ShapeDtypeStruct((n, W_C), F32)] * 2,
        compiler_params=pltpu.CompilerParams(dimension_semantics=("arbitrary",)),
        name="memkv",
    )(mem2, norm_g.reshape(1, D_MODEL), w_bf16)


def _xattn_kernel(q_ref, k_ref, v_ref, o_ref):
    q, k, v = q_ref[0], k_ref[0], v_ref[0]
    for h in range(N_HEADS_C):
        sl = slice(h * HEAD_DIM_C, (h + 1) * HEAD_DIM_C)
        s = lax.dot_general(q[:, sl].astype(BF16), k[:, sl].astype(BF16), (((1,), (1,)), ((), ())),
                            preferred_element_type=F32) * (HEAD_DIM_C ** -0.5)
        m = jnp.max(s, axis=-1, keepdims=True)
        p = jnp.exp(s - m)
        p = p / jnp.sum(p, axis=-1, keepdims=True)
        o_ref[0, :, sl] = jnp.dot(p.astype(BF16), v[:, sl].astype(BF16), preferred_element_type=F32)


def _xattn(cq3, mk3, mv3, tq):
    b, t, _ = cq3.shape
    n_mem = mk3.shape[1]
    return pl.pallas_call(
        _xattn_kernel,
        grid=(b, t // tq),
        in_specs=[pl.BlockSpec((1, tq, W_C), lambda i, j: (i, j, 0)),
                  pl.BlockSpec((1, n_mem, W_C), lambda i, j: (i, 0, 0)),
                  pl.BlockSpec((1, n_mem, W_C), lambda i, j: (i, 0, 0))],
        out_specs=pl.BlockSpec((1, tq, W_C), lambda i, j: (i, j, 0)),
        out_shape=jax.ShapeDtypeStruct((b, t, W_C), F32),
        compiler_params=pltpu.CompilerParams(dimension_semantics=("arbitrary", "arbitrary")),
        name="xattn",
    )(cq3, mk3, mv3)


def _merge_mlp_kernel(x_ref, oa_ref, ob_ref, oc_ref, gs_ref, wa_ref, wb_ref, wc_ref, wo_ref,
                      gm_ref, wu_ref, wd_ref, gf_ref, y_ref):
    def bdot(a, w):
        return jnp.dot(a.astype(BF16), w, preferred_element_type=F32)

    merged = (gs_ref[:, :D_MODEL] * bdot(oa_ref[...], wa_ref[...])
              + gs_ref[:, D_MODEL:2 * D_MODEL] * bdot(ob_ref[...], wb_ref[...])
              + gs_ref[:, 2 * D_MODEL:] * bdot(oc_ref[...], wc_ref[...]))
    h = x_ref[...] + bdot(merged, wo_ref[...])
    hn = _rmsnorm(h, gm_ref[...]).astype(BF16)
    acc = h
    for c in range(0, D_FF, 1024):
        u = jnp.dot(hn, wu_ref[:, c:c + 1024], preferred_element_type=F32)
        acc = acc + bdot(jnp.square(jnp.maximum(u, 0.0)), wd_ref[c:c + 1024, :])
    y_ref[...] = _rmsnorm(acc, gf_ref[...])


def _merge_mlp(x2, oa, ob, oc, gs, wa, wb, wc, wo, g_mlp, wu, wd, g_final, tm):
    n = x2.shape[0]
    row = lambda i: (i, 0)
    const = lambda i: (0, 0)
    resident = lambda shape: pl.BlockSpec(shape, const, pipeline_mode=pl.Buffered(1))
    return pl.pallas_call(
        _merge_mlp_kernel,
        grid=(n // tm,),
        in_specs=[
            pl.BlockSpec((tm, D_MODEL), row),
            pl.BlockSpec((tm, W_A), row), pl.BlockSpec((tm, W_B), row), pl.BlockSpec((tm, W_C), row),
            pl.BlockSpec((tm, N_BRANCH * D_MODEL), row),
            resident((W_A, D_MODEL)), resident((W_B, D_MODEL)), resident((W_C, D_MODEL)),
            resident((D_MODEL, D_MODEL)),
            pl.BlockSpec((1, D_MODEL), const),
            resident((D_MODEL, D_FF)), resident((D_FF, D_MODEL)),
            pl.BlockSpec((1, D_MODEL), const),
        ],
        out_specs=pl.BlockSpec((tm, D_MODEL), row),
        out_shape=jax.ShapeDtypeStruct((n, D_MODEL), F32),
        compiler_params=pltpu.CompilerParams(
            dimension_semantics=("arbitrary",), vmem_limit_bytes=VMEM_LIMIT_BYTES),
        name="merge_mlp",
    )(x2, oa, ob, oc, gs, wa, wb, wc, wo, g_mlp.reshape(1, -1), wu, wd, g_final.reshape(1, -1))


def _rwkv_kernel(pb_ref, sh_ref, s0_ref, mu_ref, w0_ref, wup_ref, a0_ref, aup_ref, gup_ref,
                 kk_ref, ka_ref, rk_ref, lng_ref, lnb_ref, ob_ref, sout_ref, carry_ref, state_ref,
                 *, chunk, t_valid):
    c = pl.program_id(1)
    hi = lax.Precision.HIGHEST

    @pl.when(c == 0)
    def _():
        carry_ref[...] = sh_ref[0]
        state_ref[...] = s0_ref[0]

    pbc = pb_ref[0]
    row1 = lax.broadcasted_iota(jnp.int32, (chunk, 1), 0)
    prev = jnp.where(row1 == 0, carry_ref[...], pltpu.roll(pbc, 1, axis=0))
    carry_ref[...] = pbc[chunk - 1:chunk, :]
    ps = pbc + (prev - pbc) * mu_ref[...]
    r, k, v = ps[:, :W_B], ps[:, W_B:2 * W_B], ps[:, 2 * W_B:3 * W_B]
    o = 3 * W_B
    wl = ps[:, o:o + D_DECAY_LORA]
    al = ps[:, o + D_DECAY_LORA:o + D_DECAY_LORA + D_AAA_LORA]
    gl = ps[:, o + D_DECAY_LORA + D_AAA_LORA:]

    z = -(w0_ref[...] + jnp.dot(jnp.tanh(wl), wup_ref[...], precision=hi, preferred_element_type=F32))
    softplus = jnp.maximum(z, 0.0) + jnp.log(1.0 + jnp.exp(-jnp.abs(z)))
    ld = -jnp.exp(-softplus - 0.5)
    alpha = jax.nn.sigmoid(a0_ref[...] + jnp.dot(al, aup_ref[...], precision=hi, preferred_element_type=F32))
    gate = jnp.dot(jax.nn.sigmoid(gl), gup_ref[...], precision=hi, preferred_element_type=F32)
    kkf = k * kk_ref[...]
    khf = k * (1.0 + (alpha - 1.0) * ka_ref[...])
    if t_valid is not None:
        valid = (row1 + c * chunk) < t_valid
        ld = jnp.where(valid, ld, 0.0)
        alpha = jnp.where(valid, alpha, 0.0)
        khf = jnp.where(valid, khf, 0.0)

    rr = lax.broadcasted_iota(jnp.int32, (chunk, chunk), 0)
    cc = lax.broadcasted_iota(jnp.int32, (chunk, chunk), 1)
    strict, incl = rr > cc, rr >= cc
    cum = jnp.dot(incl.astype(F32), ld, precision=hi, preferred_element_type=F32)
    ecum, einv, eprev = jnp.exp(cum), jnp.exp(-cum), jnp.exp(cum - ld)

    def mm(a, b):
        return jnp.dot(a.astype(BF16), b.astype(BF16), preferred_element_type=F32)

    def mm_t(a, b):
        return lax.dot_general(a.astype(BF16), b.astype(BF16), (((1,), (1,)), ((), ())),
                               preferred_element_type=F32)

    n_it = max(1, int(math.ceil(math.log2(chunk))))
    for h in range(N_HEADS_B):
        sl = slice(h * HEAD_DIM_B, (h + 1) * HEAD_DIM_B)
        kkh = kkf[:, sl]
        kkh = kkh / jnp.maximum(jnp.sqrt(jnp.sum(kkh * kkh, axis=-1, keepdims=True)), 1e-12)
        rh, vh, khh = r[:, sl], v[:, sl], khf[:, sl]
        a_t = -kkh * eprev[:, sl]
        r_t = rh * ecum[:, sl]
        b_t = kkh * alpha[:, sl] * einv[:, sl]
        k_t = khh * einv[:, sl]
        x1 = jnp.concatenate([a_t, r_t], axis=0)
        s0 = state_ref[h]
        g_b = mm_t(x1, b_t)
        g_k = mm_t(x1, k_t)
        p0 = mm_t(x1, s0)
        u = p0[:chunk] + mm(jnp.where(strict, g_k[:chunk], 0.0), vh)
        lp = jnp.where(strict, g_b[:chunk], 0.0)
        for it in range(n_it):
            u = u + mm(lp, u)
            if it + 1 < n_it:
                lp = mm(lp, lp)
        y = (p0[chunk:] + mm(jnp.where(incl, g_b[chunk:], 0.0), u)
             + mm(jnp.where(incl, g_k[chunk:], 0.0), vh))
        g_c = ecum[chunk - 1:chunk, sl]
        uv = jnp.concatenate([u, vh], axis=0)
        bk = jnp.concatenate([b_t, k_t], axis=0) * g_c
        state_ref[h] = s0 * g_c + lax.dot_general(
            uv.astype(BF16), bk.astype(BF16), (((0,), (0,)), ((), ())), preferred_element_type=F32)

        mean = jnp.mean(y, axis=-1, keepdims=True)
        d = y - mean
        var = jnp.mean(d * d, axis=-1, keepdims=True)
        yn = d * lax.rsqrt(var + GN_EPS) * lng_ref[:, sl] + lnb_ref[:, sl]
        bonus = jnp.sum(rh * khh * rk_ref[:, sl], axis=-1, keepdims=True) * vh
        ob_ref[0, :, sl] = (yn + bonus) * gate[:, sl]

    @pl.when(c == pl.num_programs(1) - 1)
    def _():
        sout_ref[0] = state_ref[...]


def _rwkv(pb3, shift_prev, wkv_prev, p, chunk, t_valid):
    b, t, _ = pb3.shape
    const = lambda i, j: (0, 0)
    vec = lambda n: pl.BlockSpec((1, n), const)
    state_spec = pl.BlockSpec((1, N_HEADS_B, HEAD_DIM_B, HEAD_DIM_B), lambda i, j: (i, 0, 0, 0))
    return pl.pallas_call(
        functools.partial(_rwkv_kernel, chunk=chunk, t_valid=t_valid),
        grid=(b, t // chunk),
        in_specs=[
            pl.BlockSpec((1, chunk, RWKV_PROJ), lambda i, j: (i, j, 0)),
            pl.BlockSpec((1, 1, RWKV_PROJ), lambda i, j: (i, 0, 0)),
            state_spec,
            vec(RWKV_PROJ), vec(W_B), pl.BlockSpec((D_DECAY_LORA, W_B), const),
            vec(W_B), pl.BlockSpec((D_AAA_LORA, W_B), const), pl.BlockSpec((D_GATE_LORA, W_B), const),
            vec(W_B), vec(W_B), vec(W_B), vec(W_B), vec(W_B),
        ],
        out_specs=[pl.BlockSpec((1, chunk, W_B), lambda i, j: (i, j, 0)), state_spec],
        out_shape=[jax.ShapeDtypeStruct((b, t, W_B), F32),
                   jax.ShapeDtypeStruct(wkv_prev.shape, F32)],
        scratch_shapes=[pltpu.VMEM((1, RWKV_PROJ), F32),
                        pltpu.VMEM((N_HEADS_B, HEAD_DIM_B, HEAD_DIM_B), F32)],
        compiler_params=pltpu.CompilerParams(dimension_semantics=("arbitrary", "arbitrary")),
        name="rwkv",
    )(pb3, shift_prev.reshape(b, 1, RWKV_PROJ), wkv_prev,
      p["mu"].reshape(1, -1), p["w0"].reshape(1, -1), p["w_up"], p["a0"].reshape(1, -1), p["a_up"],
      p["g_up"], p["k_k"].reshape(1, -1), p["k_a"].reshape(1, -1), p["r_k"].reshape(1, -1),
      p["ln_g"].reshape(1, -1), p["ln_b"].reshape(1, -1))


def _sort_key(s):
    b = lax.bitcast_convert_type(s + 0.0, jnp.int32)
    return b ^ (lax.shift_right_arithmetic(b, 31) & 0x7FFFFFFF)


def _kth_largest(count_ge, shape, topk):
    def body(it, t):
        cand = t + lax.shift_left(jnp.int32(1), 31 - it)
        return jnp.where(count_ge(cand) >= topk, cand, t)
    return lax.fori_loop(0, 32, body, jnp.full(shape, INT_MIN, jnp.int32))


def _tie_cutoff(count_eq_before, shape, need, n_bits):
    def body(it, x):
        cand = x + lax.shift_left(jnp.int32(1), n_bits - 1 - it)
        return jnp.where(count_eq_before(cand) < need, cand, x)
    return lax.fori_loop(0, n_bits, body, jnp.zeros(shape, jnp.int32))


def _dsa_prompt_kernel(qa_ref, iq_ref, ikwq_ref, ka_ref, va_ref, ikw_ref, o_ref,
                       key_ref, m_ref, l_ref, acc_ref, *, tq, topk, n_bits):
    i = pl.program_id(1)
    nkt = i + 1
    iq = iq_ref[0].astype(BF16)
    w_t = ikwq_ref[0].T
    kpos0 = lax.broadcasted_iota(jnp.int32, (tq, tq), 0)
    qpos = i * tq + lax.broadcasted_iota(jnp.int32, (tq, tq), 1)
    dn_t = (((1,), (1,)), ((), ()))

    def tile(kt):
        return pl.ds(pl.multiple_of(kt * tq, tq), tq)

    def scores(kt, carry):
        ik = ikw_ref[0, tile(kt), :][:, :IDX_DIM].astype(BF16)
        s = jnp.zeros((tq, tq), F32)
        for h in range(N_IDX_HEADS):
            d = lax.dot_general(ik, iq[:, h * IDX_DIM:(h + 1) * IDX_DIM], dn_t, preferred_element_type=F32)
            s = s + w_t[IDX_DIM + h:IDX_DIM + h + 1, :] * jnp.maximum(d, 0.0)
        key_ref[tile(kt), :] = jnp.where(kpos0 + kt * tq <= qpos, _sort_key(s), INT_MIN)
        return carry

    lax.fori_loop(0, nkt, scores, 0)

    def count(pred):
        def body(kt, acc):
            m = pred(key_ref[tile(kt), :], kt).astype(jnp.int32)
            return acc + jnp.sum(m.reshape(tq // SUBLANES, SUBLANES, tq), axis=0)
        acc = lax.fori_loop(0, nkt, body, jnp.zeros((SUBLANES, tq), jnp.int32))
        return jnp.sum(acc, axis=0, keepdims=True)

    thr = _kth_largest(lambda c: count(lambda blk, kt: blk >= c), (1, tq), topk)
    live = thr != INT_MIN
    need = topk - count(lambda blk, kt: blk > thr)
    cut = _tie_cutoff(lambda x: count(lambda blk, kt: (blk == thr) & (kpos0 + kt * tq < x)),
                      (1, tq), need, n_bits)

    m_ref[...] = jnp.full(m_ref.shape, -jnp.inf, F32)
    l_ref[...] = jnp.zeros(l_ref.shape, F32)
    acc_ref[...] = jnp.zeros(acc_ref.shape, F32)
    qa = qa_ref[0].astype(BF16)

    def attend(kt, carry):
        blk = key_ref[tile(kt), :]
        sel = (blk > thr) | ((blk == thr) & live & (kpos0 + kt * tq <= cut))
        kk = ka_ref[0, tile(kt), :].astype(BF16)
        vv = va_ref[0, tile(kt), :].astype(BF16)
        for h in range(N_HEADS_A):
            g = h // (N_HEADS_A // N_KV_A)
            gs = slice(g * HEAD_DIM_A, (g + 1) * HEAD_DIM_A)
            hs = slice(h * HEAD_DIM_A, (h + 1) * HEAD_DIM_A)
            s = lax.dot_general(kk[:, gs], qa[:, hs], dn_t, preferred_element_type=F32) * (HEAD_DIM_A ** -0.5)
            s = jnp.where(sel, s, -jnp.inf)
            m_old = m_ref[h:h + 1, :]
            m_new = jnp.maximum(m_old, jnp.max(s, axis=0, keepdims=True))
            m_safe = jnp.where(m_new == -jnp.inf, 0.0, m_new)
            alpha = jnp.exp(m_old - m_safe)
            p = jnp.exp(s - m_safe)
            l_ref[h:h + 1, :] = alpha * l_ref[h:h + 1, :] + jnp.sum(p, axis=0, keepdims=True)
            acc_ref[hs, :] = alpha * acc_ref[hs, :] + lax.dot_general(
                vv[:, gs], p.astype(BF16), (((0,), (0,)), ((), ())), preferred_element_type=F32)
            m_ref[h:h + 1, :] = m_new
        return carry

    lax.fori_loop(0, nkt, attend, 0)
    for h in range(N_HEADS_A):
        hs = slice(h * HEAD_DIM_A, (h + 1) * HEAD_DIM_A)
        acc_ref[hs, :] = acc_ref[hs, :] / l_ref[h:h + 1, :]
    o_ref[0] = acc_ref[...].T


def _dsa_prompt(qa3, iq3, ikw3, ka3, va3, tq):
    b, t, _ = qa3.shape
    topk = min(TOPK_MAX, t // 4)
    assert tq >= topk and t % tq == 0
    n_bits = max(1, int(math.ceil(math.log2(t))))
    qtile = lambda w: pl.BlockSpec((1, tq, w), lambda i, j: (i, j, 0))
    whole = lambda w: pl.BlockSpec((1, t, w), lambda i, j: (i, 0, 0))
    return pl.pallas_call(
        functools.partial(_dsa_prompt_kernel, tq=tq, topk=topk, n_bits=n_bits),
        grid=(b, t // tq),
        in_specs=[qtile(W_A), qtile(W_IQ), qtile(LANES), whole(W_KV), whole(W_KV), whole(LANES)],
        out_specs=qtile(W_A),
        out_shape=jax.ShapeDtypeStruct((b, t, W_A), F32),
        scratch_shapes=[pltpu.VMEM((t, tq), jnp.int32),
                        pltpu.VMEM((N_HEADS_A, tq), F32), pltpu.VMEM((N_HEADS_A, tq), F32),
                        pltpu.VMEM((W_A, tq), F32)],
        compiler_params=pltpu.CompilerParams(
            dimension_semantics=("arbitrary", "arbitrary"), vmem_limit_bytes=VMEM_LIMIT_BYTES),
        name="dsa_prompt",
    )(qa3, iq3, ikw3, ka3, va3, ikw3)


def _dsa_sample_kernel(pt_ref, qa_ref, iq_ref, ikwn_ref, kan_ref, van_ref, cik_ref, ck_ref, cv_ref,
                       o_ref, ik_buf, k_buf, v_buf, sems, *, n_pages, page, tn, topk, n_bits):
    b = pl.program_id(0)
    past = n_pages * page
    total = past + LANES

    @pl.when(b == 0)
    def _():
        ik_buf[past:, :] = jnp.zeros((LANES, IDX_DIM), F32)
        k_buf[past:, :] = jnp.zeros((LANES, W_KV), F32)
        v_buf[past:, :] = jnp.zeros((LANES, W_KV), F32)

    def page_copies(j):
        pg = pt_ref[b, j]
        rows = pl.ds(j * page, page)
        return (pltpu.make_async_copy(cik_ref.at[pg], ik_buf.at[rows, :], sems.at[0]),
                pltpu.make_async_copy(ck_ref.at[pg], k_buf.at[rows, :], sems.at[1]),
                pltpu.make_async_copy(cv_ref.at[pg], v_buf.at[rows, :], sems.at[2]))

    for j in range(n_pages):
        for cp in page_copies(j):
            cp.start()
    ik_buf[past:past + tn, :] = ikwn_ref[0][:, :IDX_DIM]
    k_buf[past:past + tn, :] = kan_ref[0]
    v_buf[past:past + tn, :] = van_ref[0]
    for j in range(n_pages):
        for cp in page_copies(j):
            cp.wait()

    dn_t = (((1,), (1,)), ((), ()))
    iq = iq_ref[0].astype(BF16)
    w = ikwn_ref[0]
    ik = ik_buf[...].astype(BF16)
    s = jnp.zeros((tn, total), F32)
    for h in range(N_IDX_HEADS):
        d = lax.dot_general(iq[:, h * IDX_DIM:(h + 1) * IDX_DIM], ik, dn_t, preferred_element_type=F32)
        s = s + w[:, IDX_DIM + h:IDX_DIM + h + 1] * jnp.maximum(d, 0.0)
    kpos = lax.broadcasted_iota(jnp.int32, (tn, total), 1)
    qpos = past + lax.broadcasted_iota(jnp.int32, (tn, total), 0)
    key = jnp.where(kpos <= qpos, _sort_key(s), INT_MIN)

    def count(m):
        return jnp.sum(m.astype(jnp.int32), axis=1, keepdims=True)

    thr = _kth_largest(lambda c: count(key >= c), (tn, 1), topk)
    live = thr != INT_MIN
    need = topk - count(key > thr)
    cut = _tie_cutoff(lambda x: count((key == thr) & (kpos < x)), (tn, 1), need, n_bits)
    sel = (key > thr) | ((key == thr) & live & (kpos <= cut))
    bias = jnp.where(sel, 0.0, -jnp.inf)

    rep = N_HEADS_A // N_KV_A
    qa = qa_ref[0].astype(BF16)
    bias_g = jnp.concatenate([bias] * rep, axis=0)
    kk = k_buf[...].astype(BF16)
    vv = v_buf[...].astype(BF16)
    for g in range(N_KV_A):
        gs = slice(g * HEAD_DIM_A, (g + 1) * HEAD_DIM_A)
        qg = jnp.concatenate(
            [qa[:, (g * rep + r) * HEAD_DIM_A:(g * rep + r + 1) * HEAD_DIM_A] for r in range(rep)], axis=0)
        sc = lax.dot_general(qg, kk[:, gs], dn_t, preferred_element_type=F32) * (HEAD_DIM_A ** -0.5) + bias_g
        m = jnp.max(sc, axis=1, keepdims=True)
        p = jnp.exp(sc - m)
        og = jnp.dot(p.astype(BF16), vv[:, gs], preferred_element_type=F32) / jnp.sum(p, axis=1, keepdims=True)
        for r in range(rep):
            h = g * rep + r
            o_ref[0, :, h * HEAD_DIM_A:(h + 1) * HEAD_DIM_A] = og[r * tn:(r + 1) * tn, :]


def _dsa_sample(qa3, iq3, ikw3, ka3, va3, cache_idx_k, cache_k2, cache_v2, page_table, t_new):
    db, tn, _ = qa3.shape
    n_pages = page_table.shape[1]
    page = cache_idx_k.shape[1]
    past = n_pages * page
    total = past + LANES
    topk = min(TOPK_MAX, (past + t_new) // 4)
    n_bits = max(1, int(math.ceil(math.log2(total))))
    new = lambda w: pl.BlockSpec((1, tn, w), lambda i, pt: (i, 0, 0))
    hbm = pl.BlockSpec(memory_space=pl.ANY)
    return pl.pallas_call(
        functools.partial(_dsa_sample_kernel, n_pages=n_pages, page=page, tn=tn, topk=topk, n_bits=n_bits),
        grid_spec=pltpu.PrefetchScalarGridSpec(
            num_scalar_prefetch=1,
            grid=(db,),
            in_specs=[new(W_A), new(W_IQ), new(LANES), new(W_KV), new(W_KV), hbm, hbm, hbm],
            out_specs=new(W_A),
            scratch_shapes=[pltpu.VMEM((total, IDX_DIM), F32), pltpu.VMEM((total, W_KV), F32),
                            pltpu.VMEM((total, W_KV), F32), pltpu.SemaphoreType.DMA((3,))],
        ),
        out_shape=jax.ShapeDtypeStruct((db, tn, W_A), F32),
        compiler_params=pltpu.CompilerParams(
            dimension_semantics=("arbitrary",), vmem_limit_bytes=VMEM_LIMIT_BYTES),
        name="dsa_sample",
    )(page_table, qa3, iq3, ikw3, ka3, va3, cache_idx_k, cache_k2, cache_v2)


def _pack_w_in(w_in):
    split = SEG_IKW[0] + IDX_DIM + N_IDX_HEADS
    pad = jnp.zeros((D_MODEL, _W_IK_PAD), w_in.dtype)
    return jnp.concatenate([w_in[:, :split], pad, w_in[:, split:]], axis=1).astype(BF16)


def _pick_tile(n, pref):
    t = min(n, pref)
    while n % t:
        t //= 2
    return t


def _group(x, pos, dsa_fn, mem_k3, mem_v3, shift_prev, wkv_prev, t_valid, wts):
    b, t, _ = x.shape
    n = b * t
    x2 = x.reshape(n, D_MODEL)
    tm = _pick_tile(n, 256)
    qa, ka, va, iq, ikw, pb, cq, gs = _inproj(x2, pos, wts["norm_mix_g"], wts["w_in"], wts["b_gate"], tm)
    r3 = lambda a: a.reshape(b, t, a.shape[-1])
    oa = dsa_fn(r3(qa), r3(iq), r3(ikw), r3(ka), r3(va))
    chunk = _pick_tile(t, 64)
    ob, wkv_new = _rwkv(r3(pb), shift_prev, wkv_prev, wts["rwkv"], chunk, t_valid)
    oc = _xattn(r3(cq), mem_k3, mem_v3, _pick_tile(t, 512))
    y = _merge_mlp(x2, oa.reshape(n, W_A), ob.reshape(n, W_B), oc.reshape(n, W_C), gs,
                   wts["w_branch_a"], wts["w_branch_b"], wts["w_branch_c"], wts["w_out"],
                   wts["norm_mlp_g"], wts["w_mlp_up"], wts["w_mlp_down"], wts["norm_final_g"], tm)
    return y.reshape(b, t, D_MODEL), r3(ka), r3(va), r3(ikw)[..., :IDX_DIM], wkv_new, r3(pb)


def kernel(x_prompt, x_sample, mem_prompt, cache_k, cache_v, cache_idx_k, cache_mem_k, cache_mem_v,
           state_wkv, state_shift, page_table, norm_mix_g, norm_mem_g, norm_mlp_g, norm_final_g,
           w_in, b_gate, w_mem_kv, rwkv_mu, rwkv_w0, rwkv_w_up, rwkv_a0, rwkv_a_up, rwkv_g_up,
           rwkv_k_k, rwkv_k_a, rwkv_r_k, rwkv_ln_g, rwkv_ln_b, w_branch_a, w_branch_b, w_branch_c,
           w_out, w_mlp_up, w_mlp_down):
    bf = lambda w: w.astype(BF16)
    wts = dict(
        norm_mix_g=norm_mix_g, norm_mlp_g=norm_mlp_g, norm_final_g=norm_final_g,
        w_in=_pack_w_in(w_in), b_gate=b_gate,
        rwkv=dict(mu=rwkv_mu, w0=rwkv_w0, w_up=rwkv_w_up, a0=rwkv_a0, a_up=rwkv_a_up, g_up=rwkv_g_up,
                  k_k=rwkv_k_k, k_a=rwkv_k_a, r_k=rwkv_r_k, ln_g=rwkv_ln_g, ln_b=rwkv_ln_b),
        w_branch_a=bf(w_branch_a), w_branch_b=bf(w_branch_b), w_branch_c=bf(w_branch_c),
        w_out=bf(w_out), w_mlp_up=bf(w_mlp_up), w_mlp_down=bf(w_mlp_down))

    b, t, _ = x_prompt.shape
    n_mem = mem_prompt.shape[1]
    mem2 = mem_prompt.reshape(b * n_mem, D_MODEL)
    mk, mv = _memkv(mem2, norm_mem_g, bf(w_mem_kv), _pick_tile(b * n_mem, 256))
    mk3, mv3 = mk.reshape(b, n_mem, W_C), mv.reshape(b, n_mem, W_C)
    shift0 = jnp.zeros((b, RWKV_PROJ), F32)
    wkv0 = jnp.zeros((b, N_HEADS_B, HEAD_DIM_B, HEAD_DIM_B), F32)
    dsa_p = functools.partial(_dsa_prompt, tq=_pick_tile(t, 256))
    y_p, k_p, v_p, ik_p, wkv_p, pb_p = _group(
        x_prompt, jnp.arange(t), dsa_p, mk3, mv3, shift0, wkv0, None, wts)

    db, tn, _ = x_sample.shape
    n_pages = page_table.shape[1]
    page = cache_idx_k.shape[1]
    past = n_pages * page
    tp = -(-tn // SUBLANES) * SUBLANES
    x_s = jnp.pad(x_sample, ((0, 0), (0, tp - tn), (0, 0)))
    n_phys = cache_k.shape[0]
    dsa_s = functools.partial(
        _dsa_sample, cache_idx_k=cache_idx_k, cache_k2=cache_k.reshape(n_phys, page, W_KV),
        cache_v2=cache_v.reshape(n_phys, page, W_KV), page_table=page_table, t_new=tn)
    y_s, k_s, v_s, ik_s, wkv_s, pb_s = _group(
        x_s, past + jnp.arange(tp), dsa_s, cache_mem_k.reshape(db, n_mem, W_C),
        cache_mem_v.reshape(db, n_mem, W_C), state_shift, state_wkv, tn, wts)

    heads = lambda a: a.reshape(a.shape[0], a.shape[1], N_KV_A, HEAD_DIM_A)
    memh = lambda a: a.reshape(b, n_mem, N_HEADS_C, HEAD_DIM_C)
    return (y_p, y_s[:, :tn], heads(k_p), heads(v_p), ik_p, memh(mk3), memh(mv3), wkv_p, pb_p[:, -1],
            heads(k_s[:, :tn]), heads(v_s[:, :tn]), ik_s[:, :tn], wkv_s, pb_s[:, tn - 1])
```

```python
import functools
import math

import jax
import jax.numpy as jnp
from jax import lax
from jax.experimental import pallas as pl
from jax.experimental.pallas import tpu as pltpu

F32 = jnp.float32
BF16 = jnp.bfloat16

D_MODEL = 1024
N_HEADS_A, N_KV_A, HEAD_DIM_A = 8, 2, 64
W_A = N_HEADS_A * HEAD_DIM_A
W_KV = N_KV_A * HEAD_DIM_A
N_IDX_HEADS, IDX_DIM = 4, 64
W_IQ = N_IDX_HEADS * IDX_DIM
TOPK_MAX = 256
ROPE_THETA = 500000.0
ROT_DIM = 16
N_HEADS_B, HEAD_DIM_B = 8, 64
W_B = N_HEADS_B * HEAD_DIM_B
D_DECAY_LORA, D_AAA_LORA, D_GATE_LORA = 64, 64, 128
RWKV_PROJ = 3 * W_B + D_DECAY_LORA + D_AAA_LORA + D_GATE_LORA
GN_EPS = 64e-5
N_HEADS_C, HEAD_DIM_C = 4, 128
W_C = N_HEADS_C * HEAD_DIM_C
N_BRANCH = 3
D_FF = 4 * D_MODEL
RMS_EPS = 1e-6

LANES = 128
SUBLANES = 8
VMEM_LIMIT_BYTES = 56 * 1024 * 1024

INT_MIN = -(2 ** 31)

_W_IK_PAD = LANES - IDX_DIM - N_IDX_HEADS
SEG_QA = (0, W_A)
SEG_KA = (W_A, W_KV)
SEG_VA = (W_A + W_KV, W_KV)
SEG_IQ = (W_A + 2 * W_KV, W_IQ)
SEG_IKW = (SEG_IQ[0] + W_IQ, LANES)
SEG_PB = (SEG_IKW[0] + LANES, RWKV_PROJ)
SEG_CQ = (SEG_PB[0] + RWKV_PROJ, W_C)
SEG_G = (SEG_CQ[0] + W_C, N_BRANCH * D_MODEL)
W_PACKED = SEG_G[0] + SEG_G[1]


def _rmsnorm(x, g):
    ms = jnp.mean(x * x, axis=-1, keepdims=True)
    return x * lax.rsqrt(ms + RMS_EPS) * g


def _rope(y, cos, s_lo, s_hi):
    return y * cos + pltpu.roll(y, LANES - ROT_DIM // 2, axis=1) * s_lo + pltpu.roll(y, ROT_DIM // 2, axis=1) * s_hi


def _inproj_kernel(x_ref, g_ref, w_ref, bg_ref, cos_ref, slo_ref, shi_ref,
                   qa_ref, ka_ref, va_ref, iq_ref, ikw_ref, pb_ref, cq_ref, gs_ref):
    xn = _rmsnorm(x_ref[...], g_ref[...]).astype(BF16)
    cos, s_lo, s_hi = cos_ref[...], slo_ref[...], shi_ref[...]

    def proj(off, width):
        return jnp.dot(xn, w_ref[:, off:off + width], preferred_element_type=F32)

    def roped(seg, out_ref):
        for c in range(0, seg[1], LANES):
            out_ref[:, c:c + LANES] = _rope(proj(seg[0] + c, LANES), cos, s_lo, s_hi)

    roped(SEG_QA, qa_ref)
    roped(SEG_KA, ka_ref)
    va_ref[...] = proj(*SEG_VA)
    roped(SEG_IQ, iq_ref)
    y = proj(*SEG_IKW)
    lane = lax.broadcasted_iota(jnp.int32, y.shape, 1)
    ikw_ref[...] = jnp.where(lane < IDX_DIM, _rope(y, cos, s_lo, s_hi), y * (W_IQ ** -0.5))
    for c in range(0, SEG_PB[1], 256):
        pb_ref[:, c:c + 256] = proj(SEG_PB[0] + c, 256)
    cq_ref[...] = proj(*SEG_CQ)
    for c in range(0, SEG_G[1], 512):
        gs_ref[:, c:c + 512] = jax.nn.sigmoid(proj(SEG_G[0] + c, 512) + bg_ref[:, c:c + 512])


def _rope_tables(pos, rows):
    half = ROT_DIM // 2
    inv_freq = ROPE_THETA ** (-jnp.arange(half, dtype=F32) / half)
    ang = pos.astype(F32)[:, None] * inv_freq[None, :]
    cos, sin = jnp.cos(ang), jnp.sin(ang)
    t = pos.shape[0]
    ones = jnp.ones((t, HEAD_DIM_A - ROT_DIM), F32)
    zeros_h = jnp.zeros((t, half), F32)
    zeros_r = jnp.zeros((t, HEAD_DIM_A - ROT_DIM), F32)
    c64 = jnp.concatenate([cos, cos, ones], axis=1)
    lo64 = jnp.concatenate([-sin, zeros_h, zeros_r], axis=1)
    hi64 = jnp.concatenate([zeros_h, sin, zeros_r], axis=1)
    reps = rows // t
    return tuple(jnp.tile(jnp.concatenate([a, a], axis=1), (reps, 1)) for a in (c64, lo64, hi64))


def _inproj(x2, pos, norm_g, w_packed, b_gate, tm):
    n = x2.shape[0]
    t = pos.shape[0]
    rows = max(t, tm)
    cos, s_lo, s_hi = _rope_tables(pos, rows)
    nt = rows // tm
    row = lambda i: (i, 0)
    const = lambda i: (0, 0)
    tab = pl.BlockSpec((tm, LANES), lambda i: (i % nt, 0))
    widths = [SEG_QA[1], SEG_KA[1], SEG_VA[1], SEG_IQ[1], SEG_IKW[1], SEG_PB[1], SEG_CQ[1], SEG_G[1]]
    return pl.pallas_call(
        _inproj_kernel,
        grid=(n // tm,),
        in_specs=[
            pl.BlockSpec((tm, D_MODEL), row),
            pl.BlockSpec((1, D_MODEL), const),
            pl.BlockSpec((D_MODEL, W_PACKED), const, pipeline_mode=pl.Buffered(1)),
            pl.BlockSpec((1, SEG_G[1]), const),
            tab, tab, tab,
        ],
        out_specs=[pl.BlockSpec((tm, w), row) for w in widths],
        out_shape=[jax.ShapeDtypeStruct((n, w), F32) for w in widths],
        compiler_params=pltpu.CompilerParams(
            dimension_semantics=("arbitrary",), vmem_limit_bytes=VMEM_LIMIT_BYTES),
        name="inproj",
    )(x2, norm_g.reshape(1, D_MODEL), w_packed, b_gate.reshape(1, -1), cos, s_lo, s_hi)


def _memkv_kernel(x_ref, g_ref, w_ref, k_ref, v_ref):
    xn = _rmsnorm(x_ref[...], g_ref[...]).astype(BF16)
    k_ref[...] = jnp.dot(xn, w_ref[:, :W_C], preferred_element_type=F32)
    v_ref[...] = jnp.dot(xn, w_ref[:, W_C:], preferred_element_type=F32)


def _memkv(mem2, norm_g, w_bf16, tm):
    n = mem2.shape[0]
    row = lambda i: (i, 0)
    const = lambda i: (0, 0)
    return pl.pallas_call(
        _memkv_kernel,
        grid=(n // tm,),
        in_specs=[pl.BlockSpec((tm, D_MODEL), row), pl.BlockSpec((1, D_MODEL), const),
                  pl.BlockSpec((D_MODEL, 2 * W_C), const)],
        out_specs=[pl.BlockSpec((tm, W_C), row)] * 2,
        out_shape=[jax.ShapeDtypeStruct((n, W_C), F32)] * 2,
        compiler_params=pltpu.CompilerParams(dimension_semantics=("arbitrary",)),
        name="memkv",
    )(mem2, norm_g.reshape(1, D_MODEL), w_bf16)


def _xattn_kernel(q_ref, k_ref, v_ref, o_ref):
    q, k, v = q_ref[0], k_ref[0], v_ref[0]
    for h in range(N_HEADS_C):
        sl = slice(h * HEAD_DIM_C, (h + 1) * HEAD_DIM_C)
        s = lax.dot_general(q[:, sl].astype(BF16), k[:, sl].astype(BF16), (((1,), (1,)), ((), ())),
                            preferred_element_type=F32) * (HEAD_DIM_C ** -0.5)
        m = jnp.max(s, axis=-1, keepdims=True)
        p = jnp.exp(s - m)
        p = p / jnp.sum(p, axis=-1, keepdims=True)
        o_ref[0, :, sl] = jnp.dot(p.astype(BF16), v[:, sl].astype(BF16), preferred_element_type=F32)


def _xattn(cq3, mk3, mv3, tq):
    b, t, _ = cq3.shape
    n_mem = mk3.shape[1]
    return pl.pallas_call(
        _xattn_kernel,
        grid=(b, t // tq),
        in_specs=[pl.BlockSpec((1, tq, W_C), lambda i, j: (i, j, 0)),
                  pl.BlockSpec((1, n_mem, W_C), lambda i, j: (i, 0, 0)),
                  pl.BlockSpec((1, n_mem, W_C), lambda i, j: (i, 0, 0))],
        out_specs=pl.BlockSpec((1, tq, W_C), lambda i, j: (i, j, 0)),
        out_shape=jax.ShapeDtypeStruct((b, t, W_C), F32),
        compiler_params=pltpu.CompilerParams(dimension_semantics=("arbitrary", "arbitrary")),
        name="xattn",
    )(cq3, mk3, mv3)


def _merge_mlp_kernel(x_ref, oa_ref, ob_ref, oc_ref, gs_ref, wa_ref, wb_ref, wc_ref, wo_ref,
                      gm_ref, wu_ref, wd_ref, gf_ref, y_ref):
    def bdot(a, w):
        return jnp.dot(a.astype(BF16), w, preferred_element_type=F32)

    merged = (gs_ref[:, :D_MODEL] * bdot(oa_ref[...], wa_ref[...])
              + gs_ref[:, D_MODEL:2 * D_MODEL] * bdot(ob_ref[...], wb_ref[...])
              + gs_ref[:, 2 * D_MODEL:] * bdot(oc_ref[...], wc_ref[...]))
    h = x_ref[...] + bdot(merged, wo_ref[...])
    hn = _rmsnorm(h, gm_ref[...]).astype(BF16)
    acc = h
    for c in range(0, D_FF, 1024):
        u = jnp.dot(hn, wu_ref[:, c:c + 1024], preferred_element_type=F32)
        acc = acc + bdot(jnp.square(jnp.maximum(u, 0.0)), wd_ref[c:c + 1024, :])
    y_ref[...] = _rmsnorm(acc, gf_ref[...])


def _merge_mlp(x2, oa, ob, oc, gs, wa, wb, wc, wo, g_mlp, wu, wd, g_final, tm):
    n = x2.shape[0]
    row = lambda i: (i, 0)
    const = lambda i: (0, 0)
    resident = lambda shape: pl.BlockSpec(shape, const, pipeline_mode=pl.Buffered(1))
    return pl.pallas_call(
        _merge_mlp_kernel,
        grid=(n // tm,),
        in_specs=[
            pl.BlockSpec((tm, D_MODEL), row),
            pl.BlockSpec((tm, W_A), row), pl.BlockSpec((tm, W_B), row), pl.BlockSpec((tm, W_C), row),
            pl.BlockSpec((tm, N_BRANCH * D_MODEL), row),
            resident((W_A, D_MODEL)), resident((W_B, D_MODEL)), resident((W_C, D_MODEL)),
            resident((D_MODEL, D_MODEL)),
            pl.BlockSpec((1, D_MODEL), const),
            resident((D_MODEL, D_FF)), resident((D_FF, D_MODEL)),
            pl.BlockSpec((1, D_MODEL), const),
        ],
        out_specs=pl.BlockSpec((tm, D_MODEL), row),
        out_shape=jax.ShapeDtypeStruct((n, D_MODEL), F32),
        compiler_params=pltpu.CompilerParams(
            dimension_semantics=("arbitrary",), vmem_limit_bytes=VMEM_LIMIT_BYTES),
        name="merge_mlp",
    )(x2, oa, ob, oc, gs, wa, wb, wc, wo, g_mlp.reshape(1, -1), wu, wd, g_final.reshape(1, -1))


def _rwkv_kernel(pb_ref, sh_ref, s0_ref, mu_ref, w0_ref, wup_ref, a0_ref, aup_ref, gup_ref,
                 kk_ref, ka_ref, rk_ref, lng_ref, lnb_ref, ob_ref, sout_ref, carry_ref, state_ref,
                 *, chunk, t_valid):
    c = pl.program_id(1)
    hi = lax.Precision.HIGHEST

    @pl.when(c == 0)
    def _():
        carry_ref[...] = sh_ref[0]
        state_ref[...] = s0_ref[0]

    pbc = pb_ref[0]
    row1 = lax.broadcasted_iota(jnp.int32, (chunk, 1), 0)
    prev = jnp.where(row1 == 0, carry_ref[...], pltpu.roll(pbc, 1, axis=0))
    carry_ref[...] = pbc[chunk - 1:chunk, :]
    ps = pbc + (prev - pbc) * mu_ref[...]
    r, k, v = ps[:, :W_B], ps[:, W_B:2 * W_B], ps[:, 2 * W_B:3 * W_B]
    o = 3 * W_B
    wl = ps[:, o:o + D_DECAY_LORA]
    al = ps[:, o + D_DECAY_LORA:o + D_DECAY_LORA + D_AAA_LORA]
    gl = ps[:, o + D_DECAY_LORA + D_AAA_LORA:]

    z = -(w0_ref[...] + jnp.dot(jnp.tanh(wl), wup_ref[...], precision=hi, preferred_element_type=F32))
    softplus = jnp.maximum(z, 0.0) + jnp.log(1.0 + jnp.exp(-jnp.abs(z)))
    ld = -jnp.exp(-softplus - 0.5)
    alpha = jax.nn.sigmoid(a0_ref[...] + jnp.dot(al, aup_ref[...], precision=hi, preferred_element_type=F32))
    gate = jnp.dot(jax.nn.sigmoid(gl), gup_ref[...], precision=hi, preferred_element_type=F32)
    kkf = k * kk_ref[...]
    khf = k * (1.0 + (alpha - 1.0) * ka_ref[...])
    if t_valid is not None:
        valid = (row1 + c * chunk) < t_valid
        ld = jnp.where(valid, ld, 0.0)
        alpha = jnp.where(valid, alpha, 0.0)
        khf = jnp.where(valid, khf, 0.0)

    rr = lax.broadcasted_iota(jnp.int32, (chunk, chunk), 0)
    cc = lax.broadcasted_iota(jnp.int32, (chunk, chunk), 1)
    strict, incl = rr > cc, rr >= cc
    cum = jnp.dot(incl.astype(F32), ld, precision=hi, preferred_element_type=F32)
    ecum, einv, eprev = jnp.exp(cum), jnp.exp(-cum), jnp.exp(cum - ld)

    def mm(a, b):
        return jnp.dot(a.astype(BF16), b.astype(BF16), preferred_element_type=F32)

    def mm_t(a, b):
        return lax.dot_general(a.astype(BF16), b.astype(BF16), (((1,), (1,)), ((), ())),
                               preferred_element_type=F32)

    n_it = max(1, int(math.ceil(math.log2(chunk))))
    heads = range(N_HEADS_B)
    sls = [slice(h * HEAD_DIM_B, (h + 1) * HEAD_DIM_B) for h in heads]
    x1, b_t, k_t = [], [], []
    for sl in sls:
        kkh = kkf[:, sl]
        kkh = kkh / jnp.maximum(jnp.sqrt(jnp.sum(kkh * kkh, axis=-1, keepdims=True)), 1e-12)
        x1.append(jnp.concatenate([-kkh * eprev[:, sl], r[:, sl] * ecum[:, sl]], axis=0).astype(BF16))
        b_t.append(kkh * alpha[:, sl] * einv[:, sl])
        k_t.append(khf[:, sl] * einv[:, sl])
    s0 = [state_ref[h] for h in heads]
    g_b = [mm_t(x1[h], b_t[h]) for h in heads]
    g_k = [mm_t(x1[h], k_t[h]) for h in heads]
    p0 = [mm_t(x1[h], s0[h]) for h in heads]
    vb = [v[:, sl].astype(BF16) for sl in sls]
    u = [p0[h][:chunk] + mm(jnp.where(strict, g_k[h][:chunk], 0.0), vb[h]) for h in heads]
    lp = [jnp.where(strict, g_b[h][:chunk], 0.0).astype(BF16) for h in heads]
    for it in range(n_it):
        u = [u[h] + mm(lp[h], u[h]) for h in heads]
        if it + 1 < n_it:
            lp = [mm(lp[h], lp[h]).astype(BF16) for h in heads]
    y = [p0[h][chunk:] + mm(jnp.where(incl, g_b[h][chunk:], 0.0), u[h])
         + mm(jnp.where(incl, g_k[h][chunk:], 0.0), vb[h]) for h in heads]
    for h, sl in enumerate(sls):
        g_c = ecum[chunk - 1:chunk, sl]
        uv = jnp.concatenate([u[h], v[:, sl]], axis=0)
        bk = jnp.concatenate([b_t[h], k_t[h]], axis=0) * g_c
        state_ref[h] = s0[h] * g_c + lax.dot_general(
            uv.astype(BF16), bk.astype(BF16), (((0,), (0,)), ((), ())), preferred_element_type=F32)
    for h, sl in enumerate(sls):
        mean = jnp.mean(y[h], axis=-1, keepdims=True)
        d = y[h] - mean
        var = jnp.mean(d * d, axis=-1, keepdims=True)
        yn = d * lax.rsqrt(var + GN_EPS) * lng_ref[:, sl] + lnb_ref[:, sl]
        bonus = jnp.sum(r[:, sl] * khf[:, sl] * rk_ref[:, sl], axis=-1, keepdims=True) * v[:, sl]
        ob_ref[0, :, sl] = (yn + bonus) * gate[:, sl]

    @pl.when(c == pl.num_programs(1) - 1)
    def _():
        sout_ref[0] = state_ref[...]


def _rwkv(pb3, shift_prev, wkv_prev, p, chunk, t_valid):
    b, t, _ = pb3.shape
    const = lambda i, j: (0, 0)
    vec = lambda n: pl.BlockSpec((1, n), const)
    state_spec = pl.BlockSpec((1, N_HEADS_B, HEAD_DIM_B, HEAD_DIM_B), lambda i, j: (i, 0, 0, 0))
    return pl.pallas_call(
        functools.partial(_rwkv_kernel, chunk=chunk, t_valid=t_valid),
        grid=(b, t // chunk),
        in_specs=[
            pl.BlockSpec((1, chunk, RWKV_PROJ), lambda i, j: (i, j, 0)),
            pl.BlockSpec((1, 1, RWKV_PROJ), lambda i, j: (i, 0, 0)),
            state_spec,
            vec(RWKV_PROJ), vec(W_B), pl.BlockSpec((D_DECAY_LORA, W_B), const),
            vec(W_B), pl.BlockSpec((D_AAA_LORA, W_B), const), pl.BlockSpec((D_GATE_LORA, W_B), const),
            vec(W_B), vec(W_B), vec(W_B), vec(W_B), vec(W_B),
        ],
        out_specs=[pl.BlockSpec((1, chunk, W_B), lambda i, j: (i, j, 0)), state_spec],
        out_shape=[jax.ShapeDtypeStruct((b, t, W_B), F32),
                   jax.ShapeDtypeStruct(wkv_prev.shape, F32)],
        scratch_shapes=[pltpu.VMEM((1, RWKV_PROJ), F32),
                        pltpu.VMEM((N_HEADS_B, HEAD_DIM_B, HEAD_DIM_B), F32)],
        compiler_params=pltpu.CompilerParams(dimension_semantics=("arbitrary", "arbitrary")),
        name="rwkv",
    )(pb3, shift_prev.reshape(b, 1, RWKV_PROJ), wkv_prev,
      p["mu"].reshape(1, -1), p["w0"].reshape(1, -1), p["w_up"], p["a0"].reshape(1, -1), p["a_up"],
      p["g_up"], p["k_k"].reshape(1, -1), p["k_a"].reshape(1, -1), p["r_k"].reshape(1, -1),
      p["ln_g"].reshape(1, -1), p["ln_b"].reshape(1, -1))


def _sort_key(s):
    b = lax.bitcast_convert_type(s + 0.0, jnp.int32)
    return b ^ (lax.shift_right_arithmetic(b, 31) & 0x7FFFFFFF)


def _kth_largest(count_ge, shape, topk):
    def body(it, t):
        cand = t + lax.shift_left(jnp.int32(1), 31 - it)
        return jnp.where(count_ge(cand) >= topk, cand, t)
    return lax.fori_loop(0, 32, body, jnp.full(shape, INT_MIN, jnp.int32))


def _tie_cutoff(count_eq_before, shape, need, n_bits):
    def body(it, x):
        cand = x + lax.shift_left(jnp.int32(1), n_bits - 1 - it)
        return jnp.where(count_eq_before(cand) < need, cand, x)
    return lax.fori_loop(0, n_bits, body, jnp.zeros(shape, jnp.int32))


def _dsa_prompt_kernel(qa_ref, iq_ref, ikwq_ref, ka_ref, va_ref, ikw_ref, o_ref,
                       key_ref, m_ref, l_ref, acc_ref, *, tq, topk):
    i = pl.program_id(1)
    nkt = i + 1
    iq = iq_ref[0]
    iq_heads = jnp.concatenate(
        [iq[:, h * IDX_DIM:(h + 1) * IDX_DIM] for h in range(N_IDX_HEADS)], axis=0).astype(BF16)
    w_t = ikwq_ref[0].T
    kpos0 = lax.broadcasted_iota(jnp.int32, (tq, tq), 0)
    qpos = i * tq + lax.broadcasted_iota(jnp.int32, (tq, tq), 1)
    dn_t = (((1,), (1,)), ((), ()))
    dn_0 = (((0,), (0,)), ((), ()))

    def tile(kt):
        return pl.ds(pl.multiple_of(kt * tq, tq), tq)

    def scores(kt, carry):
        ik = ikw_ref[0, tile(kt), :][:, :IDX_DIM].astype(BF16)
        d = lax.dot_general(ik, iq_heads, dn_t, preferred_element_type=F32)
        s = jnp.zeros((tq, tq), F32)
        for h in range(N_IDX_HEADS):
            s = s + w_t[IDX_DIM + h:IDX_DIM + h + 1, :] * jnp.maximum(d[:, h * tq:(h + 1) * tq], 0.0)
        key_ref[tile(kt), :] = jnp.where(kpos0 + kt * tq <= qpos, _sort_key(s), INT_MIN)
        return carry

    lax.fori_loop(0, nkt, scores, 0)

    def count(pred):
        def body(kt, acc):
            m = pred(key_ref[tile(kt), :], kt).astype(jnp.int32)
            return acc + jnp.sum(m.reshape(tq // SUBLANES, SUBLANES, tq), axis=0)
        acc = lax.fori_loop(0, nkt, body, jnp.zeros((SUBLANES, tq), jnp.int32))
        return jnp.sum(acc, axis=0, keepdims=True)

    thr = _kth_largest(lambda c: count(lambda blk, kt: blk >= c), (1, tq), topk)
    live = thr != INT_MIN
    need = (topk - count(lambda blk, kt: blk > thr)).astype(F32)
    before = (kpos0 > lax.broadcasted_iota(jnp.int32, (tq, tq), 1)).astype(BF16)

    m_ref[...] = jnp.full(m_ref.shape, -jnp.inf, F32)
    l_ref[...] = jnp.zeros(l_ref.shape, F32)
    acc_ref[...] = jnp.zeros(acc_ref.shape, F32)
    qa = (qa_ref[0] * (HEAD_DIM_A ** -0.5)).astype(BF16)
    ones = jnp.ones((tq, SUBLANES), BF16)
    heads = range(N_HEADS_A)
    hs = [slice(h * HEAD_DIM_A, (h + 1) * HEAD_DIM_A) for h in heads]
    gs = [slice((h // (N_HEADS_A // N_KV_A)) * HEAD_DIM_A, (h // (N_HEADS_A // N_KV_A) + 1) * HEAD_DIM_A)
          for h in heads]

    def attend(kt, tied_before):
        blk = key_ref[tile(kt), :]
        tied = (blk == thr) & live
        tied_b = jnp.where(tied, 1.0, 0.0).astype(BF16)
        rank = tied_before + jnp.dot(before, tied_b, preferred_element_type=F32)
        sel = (blk > thr) | (tied & (rank < need))
        tied_before = tied_before + lax.dot_general(ones, tied_b, dn_0, preferred_element_type=F32)[:1, :]
        bias = jnp.where(sel, 0.0, -jnp.inf)
        kk = ka_ref[0, tile(kt), :].astype(BF16)
        vv = va_ref[0, tile(kt), :].astype(BF16)
        s = [lax.dot_general(kk[:, gs[h]], qa[:, hs[h]], dn_t, preferred_element_type=F32) + bias
             for h in heads]
        p, alpha = [], []
        for h in heads:
            m_old = m_ref[h:h + 1, :]
            m_new = jnp.maximum(m_old, jnp.max(s[h], axis=0, keepdims=True))
            m_safe = jnp.where(m_new == -jnp.inf, 0.0, m_new)
            alpha.append(jnp.exp(m_old - m_safe))
            p.append(jnp.exp(s[h] - m_safe).astype(BF16))
            m_ref[h:h + 1, :] = m_new
        for h in heads:
            l_new = lax.dot_general(ones, p[h], dn_0, preferred_element_type=F32)[:1, :]
            l_ref[h:h + 1, :] = alpha[h] * l_ref[h:h + 1, :] + l_new
            acc_ref[hs[h], :] = alpha[h] * acc_ref[hs[h], :] + lax.dot_general(
                vv[:, gs[h]], p[h], dn_0, preferred_element_type=F32)
        return tied_before

    lax.fori_loop(0, nkt, attend, jnp.zeros((1, tq), F32))
    for h in heads:
        acc_ref[hs[h], :] = acc_ref[hs[h], :] / l_ref[h:h + 1, :]
    o_ref[0] = acc_ref[...].T


def _dsa_prompt(qa3, iq3, ikw3, ka3, va3, tq):
    b, t, _ = qa3.shape
    topk = min(TOPK_MAX, t // 4)
    assert tq >= topk and t % tq == 0
    qtile = lambda w: pl.BlockSpec((1, tq, w), lambda i, j: (i, j, 0))
    whole = lambda w: pl.BlockSpec((1, t, w), lambda i, j: (i, 0, 0))
    return pl.pallas_call(
        functools.partial(_dsa_prompt_kernel, tq=tq, topk=topk),
        grid=(b, t // tq),
        in_specs=[qtile(W_A), qtile(W_IQ), qtile(LANES), whole(W_KV), whole(W_KV), whole(LANES)],
        out_specs=qtile(W_A),
        out_shape=jax.ShapeDtypeStruct((b, t, W_A), F32),
        scratch_shapes=[pltpu.VMEM((t, tq), jnp.int32),
                        pltpu.VMEM((N_HEADS_A, tq), F32), pltpu.VMEM((N_HEADS_A, tq), F32),
                        pltpu.VMEM((W_A, tq), F32)],
        compiler_params=pltpu.CompilerParams(
            dimension_semantics=("arbitrary", "arbitrary"), vmem_limit_bytes=VMEM_LIMIT_BYTES),
        name="dsa_prompt",
    )(qa3, iq3, ikw3, ka3, va3, ikw3)


def _dsa_sample_kernel(pt_ref, qa_ref, iq_ref, ikwn_ref, kan_ref, van_ref, cik_ref, ck_ref, cv_ref,
                       o_ref, ik_buf, k_buf, v_buf, sems, *, n_pages, page, tn, topk, n_bits):
    b = pl.program_id(0)
    nb = pl.num_programs(0)
    past = n_pages * page
    total = past + LANES
    slot = b % 2

    def page_copies(row, dst, j):
        pg = pt_ref[row, j]
        cols = pl.ds(j * page, page)
        return (pltpu.make_async_copy(cik_ref.at[pg], ik_buf.at[dst, :, cols], sems.at[dst, 0]),
                pltpu.make_async_copy(ck_ref.at[pg], k_buf.at[dst, :, cols], sems.at[dst, 1]),
                pltpu.make_async_copy(cv_ref.at[pg], v_buf.at[dst, :, cols], sems.at[dst, 2]))

    def start_row(row, dst):
        for j in range(n_pages):
            for cp in page_copies(row, dst, j):
                cp.start()

    @pl.when(b == 0)
    def _():
        start_row(0, 0)

    @pl.when(b + 1 < nb)
    def _():
        start_row(b + 1, 1 - slot)

    for j in range(n_pages):
        for cp in page_copies(b, slot, j):
            cp.wait()

    def tail_tile(x):
        return jnp.concatenate([x, jnp.zeros((LANES - tn, x.shape[1]), F32)], axis=0).T

    ikw = ikwn_ref[0]
    ik_buf[slot, :, past:] = tail_tile(ikw)[:IDX_DIM, :]
    k_buf[slot, :, past:] = tail_tile(kan_ref[0])
    v_buf[slot, :, past:] = tail_tile(van_ref[0])

    iq = iq_ref[0]
    iq_heads = jnp.concatenate(
        [iq[:, h * IDX_DIM:(h + 1) * IDX_DIM] for h in range(N_IDX_HEADS)], axis=0).astype(BF16)
    d = jnp.dot(iq_heads, ik_buf[slot].astype(BF16), preferred_element_type=F32)
    s = jnp.zeros((tn, total), F32)
    for h in range(N_IDX_HEADS):
        s = s + ikw[:, IDX_DIM + h:IDX_DIM + h + 1] * jnp.maximum(d[h * tn:(h + 1) * tn, :], 0.0)
    kpos = lax.broadcasted_iota(jnp.int32, (tn, total), 1)
    qpos = past + lax.broadcasted_iota(jnp.int32, (tn, total), 0)
    key = jnp.where(kpos <= qpos, _sort_key(s), INT_MIN)

    def count(m):
        return jnp.sum(m.astype(jnp.int32), axis=1, keepdims=True)

    thr = _kth_largest(lambda c: count(key >= c), (tn, 1), topk)
    live = thr != INT_MIN
    need = topk - count(key > thr)
    cut = _tie_cutoff(lambda x: count((key == thr) & (kpos < x)), (tn, 1), need, n_bits)
    sel = (key > thr) | ((key == thr) & live & (kpos <= cut))
    bias = jnp.where(sel, 0.0, -jnp.inf)

    rep = N_HEADS_A // N_KV_A
    qa = (qa_ref[0] * (HEAD_DIM_A ** -0.5)).astype(BF16)
    bias_g = jnp.concatenate([bias] * rep, axis=0)
    for g in range(N_KV_A):
        gs = slice(g * HEAD_DIM_A, (g + 1) * HEAD_DIM_A)
        qg = jnp.concatenate(
            [qa[:, (g * rep + r) * HEAD_DIM_A:(g * rep + r + 1) * HEAD_DIM_A] for r in range(rep)], axis=0)
        sc = jnp.dot(qg, k_buf[slot, gs, :].astype(BF16), preferred_element_type=F32) + bias_g
        m = jnp.max(sc, axis=1, keepdims=True)
        p = jnp.exp(sc - m)
        og = lax.dot_general(p.astype(BF16), v_buf[slot, gs, :].astype(BF16), (((1,), (1,)), ((), ())),
                             preferred_element_type=F32) / jnp.sum(p, axis=1, keepdims=True)
        for r in range(rep):
            h = g * rep + r
            o_ref[0, :, h * HEAD_DIM_A:(h + 1) * HEAD_DIM_A] = og[r * tn:(r + 1) * tn, :]


def _dsa_sample(qa3, iq3, ikw3, ka3, va3, cache_idx_k, cache_k, cache_v, page_table, t_new):
    db, tn, _ = qa3.shape
    n_pages = page_table.shape[1]
    n_phys, page, _ = cache_idx_k.shape
    past = n_pages * page
    total = past + LANES
    topk = min(TOPK_MAX, (past + t_new) // 4)
    n_bits = max(1, int(math.ceil(math.log2(total))))
    cik_t = jnp.transpose(cache_idx_k, (0, 2, 1))
    ck_t = jnp.transpose(cache_k, (0, 2, 3, 1)).reshape(n_phys, W_KV, page)
    cv_t = jnp.transpose(cache_v, (0, 2, 3, 1)).reshape(n_phys, W_KV, page)
    new = lambda w: pl.BlockSpec((1, tn, w), lambda i, pt: (i, 0, 0))
    hbm = pl.BlockSpec(memory_space=pl.ANY)
    return pl.pallas_call(
        functools.partial(_dsa_sample_kernel, n_pages=n_pages, page=page, tn=tn, topk=topk, n_bits=n_bits),
        grid_spec=pltpu.PrefetchScalarGridSpec(
            num_scalar_prefetch=1,
            grid=(db,),
            in_specs=[new(W_A), new(W_IQ), new(LANES), new(W_KV), new(W_KV), hbm, hbm, hbm],
            out_specs=new(W_A),
            scratch_shapes=[pltpu.VMEM((2, IDX_DIM, total), F32), pltpu.VMEM((2, W_KV, total), F32),
                            pltpu.VMEM((2, W_KV, total), F32), pltpu.SemaphoreType.DMA((2, 3))],
        ),
        out_shape=jax.ShapeDtypeStruct((db, tn, W_A), F32),
        compiler_params=pltpu.CompilerParams(
            dimension_semantics=("arbitrary",), vmem_limit_bytes=VMEM_LIMIT_BYTES),
        name="dsa_sample",
    )(page_table, qa3, iq3, ikw3, ka3, va3, cik_t, ck_t, cv_t)


def _pack_w_in(w_in):
    split = SEG_IKW[0] + IDX_DIM + N_IDX_HEADS
    pad = jnp.zeros((D_MODEL, _W_IK_PAD), w_in.dtype)
    return jnp.concatenate([w_in[:, :split], pad, w_in[:, split:]], axis=1).astype(BF16)


def _pick_tile(n, pref):
    t = min(n, pref)
    while n % t:
        t //= 2
    return t


def _group(x, pos, dsa_fn, mem_k3, mem_v3, shift_prev, wkv_prev, t_valid, wts):
    b, t, _ = x.shape
    n = b * t
    x2 = x.reshape(n, D_MODEL)
    tm = _pick_tile(n, 256)
    qa, ka, va, iq, ikw, pb, cq, gs = _inproj(x2, pos, wts["norm_mix_g"], wts["w_in"], wts["b_gate"], tm)
    r3 = lambda a: a.reshape(b, t, a.shape[-1])
    oa = dsa_fn(r3(qa), r3(iq), r3(ikw), r3(ka), r3(va))
    chunk = _pick_tile(t, 64)
    ob, wkv_new = _rwkv(r3(pb), shift_prev, wkv_prev, wts["rwkv"], chunk, t_valid)
    oc = _xattn(r3(cq), mem_k3, mem_v3, _pick_tile(t, 512))
    y = _merge_mlp(x2, oa.reshape(n, W_A), ob.reshape(n, W_B), oc.reshape(n, W_C), gs,
                   wts["w_branch_a"], wts["w_branch_b"], wts["w_branch_c"], wts["w_out"],
                   wts["norm_mlp_g"], wts["w_mlp_up"], wts["w_mlp_down"], wts["norm_final_g"], tm)
    return y.reshape(b, t, D_MODEL), r3(ka), r3(va), r3(ikw)[..., :IDX_DIM], wkv_new, r3(pb)


def kernel(x_prompt, x_sample, mem_prompt, cache_k, cache_v, cache_idx_k, cache_mem_k, cache_mem_v,
           state_wkv, state_shift, page_table, norm_mix_g, norm_mem_g, norm_mlp_g, norm_final_g,
           w_in, b_gate, w_mem_kv, rwkv_mu, rwkv_w0, rwkv_w_up, rwkv_a0, rwkv_a_up, rwkv_g_up,
           rwkv_k_k, rwkv_k_a, rwkv_r_k, rwkv_ln_g, rwkv_ln_b, w_branch_a, w_branch_b, w_branch_c,
           w_out, w_mlp_up, w_mlp_down):
    bf = lambda w: w.astype(BF16)
    wts = dict(
        norm_mix_g=norm_mix_g, norm_mlp_g=norm_mlp_g, norm_final_g=norm_final_g,
        w_in=_pack_w_in(w_in), b_gate=b_gate,
        rwkv=dict(mu=rwkv_mu, w0=rwkv_w0, w_up=rwkv_w_up, a0=rwkv_a0, a_up=rwkv_a_up, g_up=rwkv_g_up,
                  k_k=rwkv_k_k, k_a=rwkv_k_a, r_k=rwkv_r_k, ln_g=rwkv_ln_g, ln_b=rwkv_ln_b),
        w_branch_a=bf(w_branch_a), w_branch_b=bf(w_branch_b), w_branch_c=bf(w_branch_c),
        w_out=bf(w_out), w_mlp_up=bf(w_mlp_up), w_mlp_down=bf(w_mlp_down))

    b, t, _ = x_prompt.shape
    n_mem = mem_prompt.shape[1]
    mem2 = mem_prompt.reshape(b * n_mem, D_MODEL)
    mk, mv = _memkv(mem2, norm_mem_g, bf(w_mem_kv), _pick_tile(b * n_mem, 256))
    mk3, mv3 = mk.reshape(b, n_mem, W_C), mv.reshape(b, n_mem, W_C)
    shift0 = jnp.zeros((b, RWKV_PROJ), F32)
    wkv0 = jnp.zeros((b, N_HEADS_B, HEAD_DIM_B, HEAD_DIM_B), F32)
    dsa_p = functools.partial(_dsa_prompt, tq=_pick_tile(t, 256))
    y_p, k_p, v_p, ik_p, wkv_p, pb_p = _group(
        x_prompt, jnp.arange(t), dsa_p, mk3, mv3, shift0, wkv0, None, wts)

    db, tn, _ = x_sample.shape
    n_pages = page_table.shape[1]
    page = cache_idx_k.shape[1]
    past = n_pages * page
    tp = -(-tn // SUBLANES) * SUBLANES
    x_s = jnp.pad(x_sample, ((0, 0), (0, tp - tn), (0, 0)))
    dsa_s = functools.partial(_dsa_sample, cache_idx_k=cache_idx_k, cache_k=cache_k, cache_v=cache_v,
                              page_table=page_table, t_new=tn)
    y_s, k_s, v_s, ik_s, wkv_s, pb_s = _group(
        x_s, past + jnp.arange(tp), dsa_s, cache_mem_k.reshape(db, n_mem, W_C),
        cache_mem_v.reshape(db, n_mem, W_C), state_shift, state_wkv, tn, wts)

    heads = lambda a: a.reshape(a.shape[0], a.shape[1], N_KV_A, HEAD_DIM_A)
    memh = lambda a: a.reshape(b, n_mem, N_HEADS_C, HEAD_DIM_C)
    return (y_p, y_s[:, :tn], heads(k_p), heads(v_p), ik_p, memh(mk3), memh(mv3), wkv_p, pb_p[:, -1],
            heads(k_s[:, :tn]), heads(v_s[:, :tn]), ik_s[:, :tn], wkv_s, pb_s[:, tn - 1])
```

```python
import functools
import math

import jax
import jax.numpy as jnp
from jax import lax
from jax.experimental import pallas as pl
from jax.experimental.pallas import tpu as pltpu

F32 = jnp.float32
BF16 = jnp.bfloat16

D_MODEL = 1024
N_HEADS_A, N_KV_A, HEAD_DIM_A = 8, 2, 64
W_A = N_HEADS_A * HEAD_DIM_A
W_KV = N_KV_A * HEAD_DIM_A
N_IDX_HEADS, IDX_DIM = 4, 64
W_IQ = N_IDX_HEADS * IDX_DIM
TOPK_MAX = 256
ROPE_THETA = 500000.0
ROT_DIM = 16
N_HEADS_B, HEAD_DIM_B = 8, 64
W_B = N_HEADS_B * HEAD_DIM_B
D_DECAY_LORA, D_AAA_LORA, D_GATE_LORA = 64, 64, 128
RWKV_PROJ = 3 * W_B + D_DECAY_LORA + D_AAA_LORA + D_GATE_LORA
GN_EPS = 64e-5
N_HEADS_C, HEAD_DIM_C = 4, 128
W_C = N_HEADS_C * HEAD_DIM_C
N_BRANCH = 3
D_FF = 4 * D_MODEL
RMS_EPS = 1e-6

LANES = 128
SUBLANES = 8
VMEM_LIMIT_BYTES = 56 * 1024 * 1024

INT_MIN = -(2 ** 31)
I16_MIN = -(2 ** 15)

RWKV_CHUNK = 128

_W_IK_PAD = LANES - IDX_DIM - N_IDX_HEADS
SEG_QA = (0, W_A)
SEG_KA = (W_A, W_KV)
SEG_VA = (W_A + W_KV, W_KV)
SEG_IQ = (W_A + 2 * W_KV, W_IQ)
SEG_IKW = (SEG_IQ[0] + W_IQ, LANES)
SEG_PB = (SEG_IKW[0] + LANES, RWKV_PROJ)
SEG_CQ = (SEG_PB[0] + RWKV_PROJ, W_C)
SEG_G = (SEG_CQ[0] + W_C, N_BRANCH * D_MODEL)
W_PACKED = SEG_G[0] + SEG_G[1]


def _rmsnorm(x, g):
    ms = jnp.mean(x * x, axis=-1, keepdims=True)
    return x * lax.rsqrt(ms + RMS_EPS) * g


def _rope(y, cos, s_lo, s_hi):
    return y * cos + pltpu.roll(y, LANES - ROT_DIM // 2, axis=1) * s_lo + pltpu.roll(y, ROT_DIM // 2, axis=1) * s_hi


def _inproj_kernel(x_ref, g_ref, w_ref, bg_ref, cos_ref, slo_ref, shi_ref,
                   qa_ref, ka_ref, va_ref, iq_ref, ikw_ref, pb_ref, cq_ref, gs_ref):
    xn = _rmsnorm(x_ref[...], g_ref[...]).astype(BF16)
    cos, s_lo, s_hi = cos_ref[...], slo_ref[...], shi_ref[...]

    def proj(off, width):
        return jnp.dot(xn, w_ref[:, off:off + width], preferred_element_type=F32)

    def roped(seg, out_ref):
        for c in range(0, seg[1], LANES):
            out_ref[:, c:c + LANES] = _rope(proj(seg[0] + c, LANES), cos, s_lo, s_hi)

    roped(SEG_QA, qa_ref)
    roped(SEG_KA, ka_ref)
    va_ref[...] = proj(*SEG_VA)
    roped(SEG_IQ, iq_ref)
    y = proj(*SEG_IKW)
    lane = lax.broadcasted_iota(jnp.int32, y.shape, 1)
    ikw_ref[...] = jnp.where(lane < IDX_DIM, _rope(y, cos, s_lo, s_hi), y * (W_IQ ** -0.5))
    for c in range(0, SEG_PB[1], 256):
        pb_ref[:, c:c + 256] = proj(SEG_PB[0] + c, 256)
    cq_ref[...] = proj(*SEG_CQ)
    for c in range(0, SEG_G[1], 512):
        gs_ref[:, c:c + 512] = jax.nn.sigmoid(proj(SEG_G[0] + c, 512) + bg_ref[:, c:c + 512])


def _rope_tables(pos, rows):
    half = ROT_DIM // 2
    inv_freq = ROPE_THETA ** (-jnp.arange(half, dtype=F32) / half)
    ang = pos.astype(F32)[:, None] * inv_freq[None, :]
    cos, sin = jnp.cos(ang), jnp.sin(ang)
    t = pos.shape[0]
    ones = jnp.ones((t, HEAD_DIM_A - ROT_DIM), F32)
    zeros_h = jnp.zeros((t, half), F32)
    zeros_r = jnp.zeros((t, HEAD_DIM_A - ROT_DIM), F32)
    c64 = jnp.concatenate([cos, cos, ones], axis=1)
    lo64 = jnp.concatenate([-sin, zeros_h, zeros_r], axis=1)
    hi64 = jnp.concatenate([zeros_h, sin, zeros_r], axis=1)
    reps = rows // t
    return tuple(jnp.tile(jnp.concatenate([a, a], axis=1), (reps, 1)) for a in (c64, lo64, hi64))


def _inproj(x2, pos, norm_g, w_packed, b_gate, tm):
    n = x2.shape[0]
    t = pos.shape[0]
    rows = max(t, tm)
    cos, s_lo, s_hi = _rope_tables(pos, rows)
    nt = rows // tm
    row = lambda i: (i, 0)
    const = lambda i: (0, 0)
    tab = pl.BlockSpec((tm, LANES), lambda i: (i % nt, 0))
    widths = [SEG_QA[1], SEG_KA[1], SEG_VA[1], SEG_IQ[1], SEG_IKW[1], SEG_PB[1], SEG_CQ[1], SEG_G[1]]
    return pl.pallas_call(
        _inproj_kernel,
        grid=(n // tm,),
        in_specs=[
            pl.BlockSpec((tm, D_MODEL), row),
            pl.BlockSpec((1, D_MODEL), const),
            pl.BlockSpec((D_MODEL, W_PACKED), const, pipeline_mode=pl.Buffered(1)),
            pl.BlockSpec((1, SEG_G[1]), const),
            tab, tab, tab,
        ],
        out_specs=[pl.BlockSpec((tm, w), row) for w in widths],
        out_shape=[jax.ShapeDtypeStruct((n, w), F32) for w in widths],
        compiler_params=pltpu.CompilerParams(
            dimension_semantics=("arbitrary",), vmem_limit_bytes=VMEM_LIMIT_BYTES),
        name="inproj",
    )(x2, norm_g.reshape(1, D_MODEL), w_packed, b_gate.reshape(1, -1), cos, s_lo, s_hi)


def _memkv_kernel(x_ref, g_ref, w_ref, k_ref, v_ref):
    xn = _rmsnorm(x_ref[...], g_ref[...]).astype(BF16)
    k_ref[...] = jnp.dot(xn, w_ref[:, :W_C], preferred_element_type=F32)
    v_ref[...] = jnp.dot(xn, w_ref[:, W_C:], preferred_element_type=F32)


def _memkv(mem2, norm_g, w_bf16, tm):
    n = mem2.shape[0]
    row = lambda i: (i, 0)
    const = lambda i: (0, 0)
    return pl.pallas_call(
        _memkv_kernel,
        grid=(n // tm,),
        in_specs=[pl.BlockSpec((tm, D_MODEL), row), pl.BlockSpec((1, D_MODEL), const),
                  pl.BlockSpec((D_MODEL, 2 * W_C), const)],
        out_specs=[pl.BlockSpec((tm, W_C), row)] * 2,
        out_shape=[jax.ShapeDtypeStruct((n, W_C), F32)] * 2,
        compiler_params=pltpu.CompilerParams(dimension_semantics=("arbitrary",)),
        name="memkv",
    )(mem2, norm_g.reshape(1, D_MODEL), w_bf16)


def _xattn_kernel(q_ref, k_ref, v_ref, o_ref):
    q, k, v = q_ref[0], k_ref[0], v_ref[0]
    for h in range(N_HEADS_C):
        sl = slice(h * HEAD_DIM_C, (h + 1) * HEAD_DIM_C)
        s = lax.dot_general(q[:, sl].astype(BF16), k[:, sl].astype(BF16), (((1,), (1,)), ((), ())),
                            preferred_element_type=F32) * (HEAD_DIM_C ** -0.5)
        m = jnp.max(s, axis=-1, keepdims=True)
        p = jnp.exp(s - m)
        p = p / jnp.sum(p, axis=-1, keepdims=True)
        o_ref[0, :, sl] = jnp.dot(p.astype(BF16), v[:, sl].astype(BF16), preferred_element_type=F32)


def _xattn(cq3, mk3, mv3, tq):
    b, t, _ = cq3.shape
    n_mem = mk3.shape[1]
    return pl.pallas_call(
        _xattn_kernel,
        grid=(b, t // tq),
        in_specs=[pl.BlockSpec((1, tq, W_C), lambda i, j: (i, j, 0)),
                  pl.BlockSpec((1, n_mem, W_C), lambda i, j: (i, 0, 0)),
                  pl.BlockSpec((1, n_mem, W_C), lambda i, j: (i, 0, 0))],
        out_specs=pl.BlockSpec((1, tq, W_C), lambda i, j: (i, j, 0)),
        out_shape=jax.ShapeDtypeStruct((b, t, W_C), F32),
        compiler_params=pltpu.CompilerParams(dimension_semantics=("arbitrary", "arbitrary")),
        name="xattn",
    )(cq3, mk3, mv3)


def _merge_mlp_kernel(x_ref, oa_ref, ob_ref, oc_ref, gs_ref, wa_ref, wb_ref, wc_ref, wo_ref,
                      gm_ref, wu_ref, wd_ref, gf_ref, y_ref):
    def bdot(a, w):
        return jnp.dot(a.astype(BF16), w, preferred_element_type=F32)

    merged = (gs_ref[:, :D_MODEL] * bdot(oa_ref[...], wa_ref[...])
              + gs_ref[:, D_MODEL:2 * D_MODEL] * bdot(ob_ref[...], wb_ref[...])
              + gs_ref[:, 2 * D_MODEL:] * bdot(oc_ref[...], wc_ref[...]))
    h = x_ref[...] + bdot(merged, wo_ref[...])
    hn = _rmsnorm(h, gm_ref[...]).astype(BF16)
    acc = h
    for c in range(0, D_FF, 1024):
        u = jnp.dot(hn, wu_ref[:, c:c + 1024], preferred_element_type=F32)
        acc = acc + bdot(jnp.square(jnp.maximum(u, 0.0)), wd_ref[c:c + 1024, :])
    y_ref[...] = _rmsnorm(acc, gf_ref[...])


def _merge_mlp(x2, oa, ob, oc, gs, wa, wb, wc, wo, g_mlp, wu, wd, g_final, tm):
    n = x2.shape[0]
    row = lambda i: (i, 0)
    const = lambda i: (0, 0)
    resident = lambda shape: pl.BlockSpec(shape, const, pipeline_mode=pl.Buffered(1))
    return pl.pallas_call(
        _merge_mlp_kernel,
        grid=(n // tm,),
        in_specs=[
            pl.BlockSpec((tm, D_MODEL), row),
            pl.BlockSpec((tm, W_A), row), pl.BlockSpec((tm, W_B), row), pl.BlockSpec((tm, W_C), row),
            pl.BlockSpec((tm, N_BRANCH * D_MODEL), row),
            resident((W_A, D_MODEL)), resident((W_B, D_MODEL)), resident((W_C, D_MODEL)),
            resident((D_MODEL, D_MODEL)),
            pl.BlockSpec((1, D_MODEL), const),
            resident((D_MODEL, D_FF)), resident((D_FF, D_MODEL)),
            pl.BlockSpec((1, D_MODEL), const),
        ],
        out_specs=pl.BlockSpec((tm, D_MODEL), row),
        out_shape=jax.ShapeDtypeStruct((n, D_MODEL), F32),
        compiler_params=pltpu.CompilerParams(
            dimension_semantics=("arbitrary",), vmem_limit_bytes=VMEM_LIMIT_BYTES),
        name="merge_mlp",
    )(x2, oa, ob, oc, gs, wa, wb, wc, wo, g_mlp.reshape(1, -1), wu, wd, g_final.reshape(1, -1))


def _bf16_terms(x, n):
    terms = []
    for _ in range(n):
        t = x.astype(BF16)
        terms.append(t)
        x = x - t.astype(F32)
    return terms


def _dot_split(a, b):
    (ah, al), (bh, bl) = _bf16_terms(a, 2), _bf16_terms(b, 2)
    d = lambda x, y: jnp.dot(x, y, preferred_element_type=F32)
    return d(ah, bh) + (d(ah, bl) + d(al, bh))


def _head_sums(x):
    lo = lax.broadcasted_iota(jnp.int32, (x.shape[0], LANES), 1) < HEAD_DIM_B
    out = []
    for c in range(0, x.shape[1], LANES):
        xc = x[:, c:c + LANES]
        s_lo = jnp.sum(jnp.where(lo, xc, 0.0), axis=-1, keepdims=True)
        s_hi = jnp.sum(jnp.where(lo, 0.0, xc), axis=-1, keepdims=True)
        out.append(jnp.where(lo, s_lo, s_hi))
    return jnp.concatenate(out, axis=1)


def _rwkv_kernel(pb_ref, sh_ref, s0_ref, mu_ref, w0_ref, wup_ref, a0_ref, aup_ref, gup_ref,
                 kk_ref, ka_ref, rk_ref, lng_ref, lnb_ref, ob_ref, sout_ref, carry_ref, state_ref,
                 *, chunk, t_valid):
    c = pl.program_id(1)

    @pl.when(c == 0)
    def _():
        carry_ref[...] = sh_ref[0]
        state_ref[...] = s0_ref[0]

    pbc = pb_ref[0]
    row1 = lax.broadcasted_iota(jnp.int32, (chunk, 1), 0)
    prev = jnp.where(row1 == 0, carry_ref[...], pltpu.roll(pbc, 1, axis=0))
    carry_ref[...] = pbc[chunk - 1:chunk, :]
    ps = pbc + (prev - pbc) * mu_ref[...]
    r, k, v = ps[:, :W_B], ps[:, W_B:2 * W_B], ps[:, 2 * W_B:3 * W_B]
    o = 3 * W_B
    wl = ps[:, o:o + D_DECAY_LORA]
    al = ps[:, o + D_DECAY_LORA:o + D_DECAY_LORA + D_AAA_LORA]
    gl = ps[:, o + D_DECAY_LORA + D_AAA_LORA:]

    z = -(w0_ref[...] + _dot_split(jnp.tanh(wl), wup_ref[...]))
    softplus = jnp.maximum(z, 0.0) + jnp.log(1.0 + jnp.exp(-jnp.abs(z)))
    ld = -jnp.exp(-softplus - 0.5)
    alpha = jax.nn.sigmoid(a0_ref[...] + _dot_split(al, aup_ref[...]))
    gate = _dot_split(jax.nn.sigmoid(gl), gup_ref[...])
    kkf = k * kk_ref[...]
    khf = k * (1.0 + (alpha - 1.0) * ka_ref[...])
    if t_valid is not None:
        valid = (row1 + c * chunk) < t_valid
        ld = jnp.where(valid, ld, 0.0)
        alpha = jnp.where(valid, alpha, 0.0)
        khf = jnp.where(valid, khf, 0.0)

    rr = lax.broadcasted_iota(jnp.int32, (chunk, chunk), 0)
    cc = lax.broadcasted_iota(jnp.int32, (chunk, chunk), 1)
    strict, incl = rr > cc, rr >= cc
    tri = incl.astype(BF16)
    cum = sum(jnp.dot(tri, part, preferred_element_type=F32) for part in _bf16_terms(ld, 3))
    ecum, einv, eprev = jnp.exp(cum), jnp.exp(-cum), jnp.exp(cum - ld)

    def mm(a, b):
        return jnp.dot(a.astype(BF16), b.astype(BF16), preferred_element_type=F32)

    def mm_t(a, b):
        return lax.dot_general(a.astype(BF16), b.astype(BF16), (((1,), (1,)), ((), ())),
                               preferred_element_type=F32)

    kkn = kkf * lax.rsqrt(jnp.maximum(_head_sums(kkf * kkf), 1e-24))
    a_t = (-kkn * eprev).astype(BF16)
    r_t = (r * ecum).astype(BF16)
    b_t = kkn * alpha * einv
    k_t = khf * einv
    g_c = ecum[chunk - 1:chunk, :]
    b_c, k_c = (b_t * g_c).astype(BF16), (k_t * g_c).astype(BF16)
    b_t, k_t, vb = b_t.astype(BF16), k_t.astype(BF16), v.astype(BF16)

    n_it = max(1, int(math.ceil(math.log2(chunk))))
    heads = range(N_HEADS_B)
    sls = [slice(h * HEAD_DIM_B, (h + 1) * HEAD_DIM_B) for h in heads]
    x1 = [jnp.concatenate([a_t[:, sl], r_t[:, sl]], axis=0) for sl in sls]
    s0 = [state_ref[h] for h in heads]
    p0 = [mm_t(x1[h], s0[h]) for h in heads]
    if chunk % LANES == 0:
        x2 = [jnp.concatenate([b_t[:, sl], k_t[:, sl]], axis=0) for sl in sls]
        g = [mm_t(x1[h], x2[h]) for h in heads]
        incl2 = jnp.concatenate([incl, incl], axis=1)
        u = [p0[h][:chunk] + mm(jnp.where(strict, g[h][:chunk, chunk:], 0.0), vb[:, sl])
             for h, sl in enumerate(sls)]
        lp = [jnp.where(strict, g[h][:chunk, :chunk], 0.0).astype(BF16) for h in heads]
        for it in range(n_it - 1):
            sq = [mm(lp[h], jnp.concatenate([lp[h], u[h].astype(BF16)], axis=1)) for h in heads]
            u = [u[h] + sq[h][:, chunk:] for h in heads]
            lp = [sq[h][:, :chunk].astype(BF16) for h in heads]
        u = [u[h] + mm(lp[h], u[h]) for h in heads]
        uv = [jnp.concatenate([u[h].astype(BF16), vb[:, sl]], axis=0) for h, sl in enumerate(sls)]
        y = [p0[h][chunk:] + mm(jnp.where(incl2, g[h][chunk:], 0.0), uv[h]) for h in heads]
    else:
        g_b = [mm_t(x1[h], b_t[:, sl]) for h, sl in enumerate(sls)]
        g_k = [mm_t(x1[h], k_t[:, sl]) for h, sl in enumerate(sls)]
        u = [p0[h][:chunk] + mm(jnp.where(strict, g_k[h][:chunk], 0.0), vb[:, sl])
             for h, sl in enumerate(sls)]
        lp = [jnp.where(strict, g_b[h][:chunk], 0.0).astype(BF16) for h in heads]
        for it in range(n_it):
            u = [u[h] + mm(lp[h], u[h]) for h in heads]
            if it + 1 < n_it:
                lp = [mm(lp[h], lp[h]).astype(BF16) for h in heads]
        y = [p0[h][chunk:] + mm(jnp.where(incl, g_b[h][chunk:], 0.0), u[h])
             + mm(jnp.where(incl, g_k[h][chunk:], 0.0), vb[:, sl]) for h, sl in enumerate(sls)]
        uv = [jnp.concatenate([u[h].astype(BF16), vb[:, sl]], axis=0) for h, sl in enumerate(sls)]
    for h, sl in enumerate(sls):
        bk = jnp.concatenate([b_c[:, sl], k_c[:, sl]], axis=0)
        state_ref[h] = s0[h] * g_c[:, sl] + lax.dot_general(
            uv[h], bk, (((0,), (0,)), ((), ())), preferred_element_type=F32)

    y_all = jnp.concatenate(y, axis=1)
    d = y_all - _head_sums(y_all) * (1.0 / HEAD_DIM_B)
    var = _head_sums(d * d) * (1.0 / HEAD_DIM_B)
    yn = d * lax.rsqrt(var + GN_EPS) * lng_ref[...] + lnb_ref[...]
    bonus = _head_sums(r * khf * rk_ref[...]) * v
    ob_ref[0] = (yn + bonus) * gate

    @pl.when(c == pl.num_programs(1) - 1)
    def _():
        sout_ref[0] = state_ref[...]


def _rwkv(pb3, shift_prev, wkv_prev, p, chunk, t_valid):
    b, t, _ = pb3.shape
    const = lambda i, j: (0, 0)
    vec = lambda n: pl.BlockSpec((1, n), const)
    state_spec = pl.BlockSpec((1, N_HEADS_B, HEAD_DIM_B, HEAD_DIM_B), lambda i, j: (i, 0, 0, 0))
    return pl.pallas_call(
        functools.partial(_rwkv_kernel, chunk=chunk, t_valid=t_valid),
        grid=(b, t // chunk),
        in_specs=[
            pl.BlockSpec((1, chunk, RWKV_PROJ), lambda i, j: (i, j, 0)),
            pl.BlockSpec((1, 1, RWKV_PROJ), lambda i, j: (i, 0, 0)),
            state_spec,
            vec(RWKV_PROJ), vec(W_B), pl.BlockSpec((D_DECAY_LORA, W_B), const),
            vec(W_B), pl.BlockSpec((D_AAA_LORA, W_B), const), pl.BlockSpec((D_GATE_LORA, W_B), const),
            vec(W_B), vec(W_B), vec(W_B), vec(W_B), vec(W_B),
        ],
        out_specs=[pl.BlockSpec((1, chunk, W_B), lambda i, j: (i, j, 0)), state_spec],
        out_shape=[jax.ShapeDtypeStruct((b, t, W_B), F32),
                   jax.ShapeDtypeStruct(wkv_prev.shape, F32)],
        scratch_shapes=[pltpu.VMEM((1, RWKV_PROJ), F32),
                        pltpu.VMEM((N_HEADS_B, HEAD_DIM_B, HEAD_DIM_B), F32)],
        compiler_params=pltpu.CompilerParams(dimension_semantics=("arbitrary", "arbitrary")),
        name="rwkv",
    )(pb3, shift_prev.reshape(b, 1, RWKV_PROJ), wkv_prev,
      p["mu"].reshape(1, -1), p["w0"].reshape(1, -1), p["w_up"], p["a0"].reshape(1, -1), p["a_up"],
      p["g_up"], p["k_k"].reshape(1, -1), p["k_a"].reshape(1, -1), p["r_k"].reshape(1, -1),
      p["ln_g"].reshape(1, -1), p["ln_b"].reshape(1, -1))


def _sort_key(s):
    b = lax.bitcast_convert_type(s + 0.0, jnp.int32)
    return b ^ (lax.shift_right_arithmetic(b, 31) & 0x7FFFFFFF)


def _kth_largest(count_ge, shape, topk):
    def body(it, t):
        cand = t + lax.shift_left(jnp.int32(1), 31 - it)
        return jnp.where(count_ge(cand) >= topk, cand, t)
    return lax.fori_loop(0, 32, body, jnp.full(shape, INT_MIN, jnp.int32))


def _tie_cutoff(count_eq_before, shape, need, n_bits):
    def body(it, x):
        cand = x + lax.shift_left(jnp.int32(1), n_bits - 1 - it)
        return jnp.where(count_eq_before(cand) < need, cand, x)
    return lax.fori_loop(0, n_bits, body, jnp.zeros(shape, jnp.int32))


def _dsa_prompt_kernel(qa_ref, iq_ref, ikwq_ref, ka_ref, va_ref, ikw_ref, o_ref,
                       key_ref, kh_ref, kl_ref, m_ref, l_ref, acc_ref, *, tq, tk, topk):
    i = pl.program_id(1)
    nkt = ((i + 1) * tq + tk - 1) // tk
    iq = iq_ref[0]
    iq_heads = jnp.concatenate(
        [iq[:, h * IDX_DIM:(h + 1) * IDX_DIM] for h in range(N_IDX_HEADS)], axis=0).astype(BF16)
    w_t = ikwq_ref[0].T
    kpos0 = lax.broadcasted_iota(jnp.int32, (tk, tq), 0)
    qpos = i * tq + lax.broadcasted_iota(jnp.int32, (tk, tq), 1)
    dn_t = (((1,), (1,)), ((), ()))
    dn_0 = (((0,), (0,)), ((), ()))

    def tile(kt):
        return pl.ds(pl.multiple_of(kt * tk, tk), tk)

    def scores(kt, carry):
        ik = ikw_ref[0, tile(kt), :][:, :IDX_DIM].astype(BF16)
        d = lax.dot_general(ik, iq_heads, dn_t, preferred_element_type=F32)
        s = jnp.zeros((tk, tq), F32)
        for h in range(N_IDX_HEADS):
            s = s + w_t[IDX_DIM + h:IDX_DIM + h + 1, :] * jnp.maximum(d[:, h * tq:(h + 1) * tq], 0.0)
        key = jnp.where(kpos0 + kt * tk <= qpos, _sort_key(s), INT_MIN)
        key_ref[tile(kt), :] = key
        kh_ref[tile(kt), :] = lax.shift_right_arithmetic(key, 16).astype(jnp.int16)
        kl_ref[tile(kt), :] = ((key & 0xFFFF) + I16_MIN).astype(jnp.int16)
        return carry

    lax.fori_loop(0, nkt, scores, 0)

    rows16 = 2 * SUBLANES

    def count16(ref, pred):
        def body(kt, acc):
            one = jnp.where(pred(ref[tile(kt), :]), jnp.int16(1), jnp.int16(0))
            for j in range(0, tk, rows16):
                acc = acc + one[j:j + rows16, :]
            return acc
        acc = lax.fori_loop(0, nkt, body, jnp.zeros((rows16, tq), jnp.int16))
        return jnp.sum(acc.astype(jnp.int32), axis=0, keepdims=True)

    def kth16(ref, above):
        def body(it, t):
            cand = t + lax.shift_left(jnp.int32(1), 15 - it)
            c16 = cand.astype(jnp.int16)
            return jnp.where(above + count16(ref, lambda blk: blk >= c16) >= topk, cand, t)
        return lax.fori_loop(0, 16, body, jnp.full((1, tq), I16_MIN, jnp.int32))

    t_hi = kth16(kh_ref, 0)
    h16 = t_hi.astype(jnp.int16)
    n_hi = count16(kh_ref, lambda blk: blk > h16)

    def keep_equal(kt, carry):
        kl_ref[tile(kt), :] = jnp.where(kh_ref[tile(kt), :] == h16, kl_ref[tile(kt), :], jnp.int16(I16_MIN))
        return carry

    lax.fori_loop(0, nkt, keep_equal, 0)
    t_lo = kth16(kl_ref, n_hi)
    l16 = t_lo.astype(jnp.int16)
    thr = lax.shift_left(t_hi, 16) + (t_lo - I16_MIN)
    live = thr != INT_MIN
    need = (topk - n_hi - count16(kl_ref, lambda blk: blk > l16)).astype(F32)
    before = (lax.broadcasted_iota(jnp.int32, (tk, tk), 0)
              > lax.broadcasted_iota(jnp.int32, (tk, tk), 1)).astype(BF16)
    ones = jnp.ones((tk, SUBLANES), BF16)

    def select(kt, tied_before):
        blk = key_ref[tile(kt), :]
        tied = (blk == thr) & live
        tied_b = jnp.where(tied, 1.0, 0.0).astype(BF16)
        rank = tied_before + jnp.dot(before, tied_b, preferred_element_type=F32)
        sel = (blk > thr) | (tied & (rank < need))
        key_ref[tile(kt), :] = lax.bitcast_convert_type(jnp.where(sel, 0.0, -jnp.inf), jnp.int32)
        return tied_before + lax.dot_general(ones, tied_b, dn_0, preferred_element_type=F32)[:1, :]

    lax.fori_loop(0, nkt, select, jnp.zeros((1, tq), F32))

    m_ref[...] = jnp.full(m_ref.shape, -jnp.inf, F32)
    l_ref[...] = jnp.zeros(l_ref.shape, F32)
    acc_ref[...] = jnp.zeros(acc_ref.shape, F32)
    qa = (qa_ref[0] * (HEAD_DIM_A ** -0.5 * math.log2(math.e))).astype(BF16)
    heads = range(N_HEADS_A)
    hs = [slice(h * HEAD_DIM_A, (h + 1) * HEAD_DIM_A) for h in heads]
    gs = [slice((h // (N_HEADS_A // N_KV_A)) * HEAD_DIM_A, (h // (N_HEADS_A // N_KV_A) + 1) * HEAD_DIM_A)
          for h in heads]

    def attend(kt, carry):
        bias = lax.bitcast_convert_type(key_ref[tile(kt), :], F32)
        kk = ka_ref[0, tile(kt), :].astype(BF16)
        vv = va_ref[0, tile(kt), :].astype(BF16)
        s = [lax.dot_general(kk[:, gs[h]], qa[:, hs[h]], dn_t, preferred_element_type=F32) + bias
             for h in heads]
        for h in heads:
            m_old = m_ref[h:h + 1, :]
            m_new = jnp.maximum(m_old, jnp.max(s[h], axis=0, keepdims=True))
            m_safe = jnp.where(m_new == -jnp.inf, 0.0, m_new)
            alpha = jnp.exp2(m_old - m_safe)
            p = jnp.exp2((s[h] - m_safe).astype(BF16))
            m_ref[h:h + 1, :] = m_new
            l_new = lax.dot_general(ones, p, dn_0, preferred_element_type=F32)[:1, :]
            l_ref[h:h + 1, :] = alpha * l_ref[h:h + 1, :] + l_new
            acc_ref[hs[h], :] = alpha * acc_ref[hs[h], :] + lax.dot_general(
                vv[:, gs[h]], p, dn_0, preferred_element_type=F32)
        return carry

    lax.fori_loop(0, nkt, attend, 0)
    for h in heads:
        acc_ref[hs[h], :] = acc_ref[hs[h], :] / l_ref[h:h + 1, :]
    o_ref[0] = acc_ref[...].T


def _dsa_prompt(qa3, iq3, ikw3, ka3, va3, tq):
    b, t, _ = qa3.shape
    topk = min(TOPK_MAX, t // 4)
    assert tq >= topk and t % tq == 0
    tk = 2 * tq if t % (2 * tq) == 0 else tq
    qtile = lambda w: pl.BlockSpec((1, tq, w), lambda i, j: (i, j, 0))
    whole = lambda w: pl.BlockSpec((1, t, w), lambda i, j: (i, 0, 0))
    return pl.pallas_call(
        functools.partial(_dsa_prompt_kernel, tq=tq, tk=tk, topk=topk),
        grid=(b, t // tq),
        in_specs=[qtile(W_A), qtile(W_IQ), qtile(LANES), whole(W_KV), whole(W_KV), whole(LANES)],
        out_specs=qtile(W_A),
        out_shape=jax.ShapeDtypeStruct((b, t, W_A), F32),
        scratch_shapes=[pltpu.VMEM((t, tq), jnp.int32),
                        pltpu.VMEM((t, tq), jnp.int16), pltpu.VMEM((t, tq), jnp.int16),
                        pltpu.VMEM((N_HEADS_A, tq), F32), pltpu.VMEM((N_HEADS_A, tq), F32),
                        pltpu.VMEM((W_A, tq), F32)],
        compiler_params=pltpu.CompilerParams(
            dimension_semantics=("arbitrary", "arbitrary"), vmem_limit_bytes=VMEM_LIMIT_BYTES),
        name="dsa_prompt",
    )(qa3, iq3, ikw3, ka3, va3, ikw3)


def _dsa_sample_kernel(pt_ref, qa_ref, iq_ref, ikwn_ref, kan_ref, van_ref, cik_ref, ck_ref, cv_ref,
                       o_ref, ik_buf, k_buf, v_buf, sems, *, n_pages, page, tn, topk, n_bits):
    b = pl.program_id(0)
    nb = pl.num_programs(0)
    past = n_pages * page
    total = past + LANES
    slot = b % 2

    def page_copies(row, dst, j):
        pg = pt_ref[row, j]
        cols = pl.ds(j * page, page)
        return (pltpu.make_async_copy(cik_ref.at[pg], ik_buf.at[dst, :, cols], sems.at[dst, 0]),
                pltpu.make_async_copy(ck_ref.at[pg], k_buf.at[dst, :, cols], sems.at[dst, 1]),
                pltpu.make_async_copy(cv_ref.at[pg], v_buf.at[dst, :, cols], sems.at[dst, 2]))

    def start_row(row, dst):
        for j in range(n_pages):
            for cp in page_copies(row, dst, j):
                cp.start()

    @pl.when(b == 0)
    def _():
        start_row(0, 0)

    @pl.when(b + 1 < nb)
    def _():
        start_row(b + 1, 1 - slot)

    for j in range(n_pages):
        for cp in page_copies(b, slot, j):
            cp.wait()

    def tail_tile(x):
        return jnp.concatenate([x, jnp.zeros((LANES - tn, x.shape[1]), F32)], axis=0).T

    ikw = ikwn_ref[0]
    ik_buf[slot, :, past:] = tail_tile(ikw)[:IDX_DIM, :]
    k_buf[slot, :, past:] = tail_tile(kan_ref[0])
    v_buf[slot, :, past:] = tail_tile(van_ref[0])

    iq = iq_ref[0]
    iq_heads = jnp.concatenate(
        [iq[:, h * IDX_DIM:(h + 1) * IDX_DIM] for h in range(N_IDX_HEADS)], axis=0).astype(BF16)
    d = jnp.dot(iq_heads, ik_buf[slot].astype(BF16), preferred_element_type=F32)
    s = jnp.zeros((tn, total), F32)
    for h in range(N_IDX_HEADS):
        s = s + ikw[:, IDX_DIM + h:IDX_DIM + h + 1] * jnp.maximum(d[h * tn:(h + 1) * tn, :], 0.0)
    kpos = lax.broadcasted_iota(jnp.int32, (tn, total), 1)
    qpos = past + lax.broadcasted_iota(jnp.int32, (tn, total), 0)
    key = jnp.where(kpos <= qpos, _sort_key(s), INT_MIN)

    def count(m):
        return jnp.sum(m.astype(jnp.int32), axis=1, keepdims=True)

    thr = _kth_largest(lambda c: count(key >= c), (tn, 1), topk)
    live = thr != INT_MIN
    need = topk - count(key > thr)
    cut = _tie_cutoff(lambda x: count((key == thr) & (kpos < x)), (tn, 1), need, n_bits)
    sel = (key > thr) | ((key == thr) & live & (kpos <= cut))
    bias = jnp.where(sel, 0.0, -jnp.inf)

    rep = N_HEADS_A // N_KV_A
    qa = (qa_ref[0] * (HEAD_DIM_A ** -0.5)).astype(BF16)
    bias_g = jnp.concatenate([bias] * rep, axis=0)
    for g in range(N_KV_A):
        gs = slice(g * HEAD_DIM_A, (g + 1) * HEAD_DIM_A)
        qg = jnp.concatenate(
            [qa[:, (g * rep + r) * HEAD_DIM_A:(g * rep + r + 1) * HEAD_DIM_A] for r in range(rep)], axis=0)
        sc = jnp.dot(qg, k_buf[slot, gs, :].astype(BF16), preferred_element_type=F32) + bias_g
        m = jnp.max(sc, axis=1, keepdims=True)
        p = jnp.exp(sc - m)
        og = lax.dot_general(p.astype(BF16), v_buf[slot, gs, :].astype(BF16), (((1,), (1,)), ((), ())),
                             preferred_element_type=F32) / jnp.sum(p, axis=1, keepdims=True)
        for r in range(rep):
            h = g * rep + r
            o_ref[0, :, h * HEAD_DIM_A:(h + 1) * HEAD_DIM_A] = og[r * tn:(r + 1) * tn, :]


def _dsa_sample(qa3, iq3, ikw3, ka3, va3, cache_idx_k, cache_k, cache_v, page_table, t_new):
    db, tn, _ = qa3.shape
    n_pages = page_table.shape[1]
    n_phys, page, _ = cache_idx_k.shape
    past = n_pages * page
    total = past + LANES
    topk = min(TOPK_MAX, (past + t_new) // 4)
    n_bits = max(1, int(math.ceil(math.log2(total))))
    cik_t = jnp.transpose(cache_idx_k, (0, 2, 1))
    ck_t = jnp.transpose(cache_k, (0, 2, 3, 1)).reshape(n_phys, W_KV, page)
    cv_t = jnp.transpose(cache_v, (0, 2, 3, 1)).reshape(n_phys, W_KV, page)
    new = lambda w: pl.BlockSpec((1, tn, w), lambda i, pt: (i, 0, 0))
    hbm = pl.BlockSpec(memory_space=pl.ANY)
    return pl.pallas_call(
        functools.partial(_dsa_sample_kernel, n_pages=n_pages, page=page, tn=tn, topk=topk, n_bits=n_bits),
        grid_spec=pltpu.PrefetchScalarGridSpec(
            num_scalar_prefetch=1,
            grid=(db,),
            in_specs=[new(W_A), new(W_IQ), new(LANES), new(W_KV), new(W_KV), hbm, hbm, hbm],
            out_specs=new(W_A),
            scratch_shapes=[pltpu.VMEM((2, IDX_DIM, total), F32), pltpu.VMEM((2, W_KV, total), F32),
                            pltpu.VMEM((2, W_KV, total), F32), pltpu.SemaphoreType.DMA((2, 3))],
        ),
        out_shape=jax.ShapeDtypeStruct((db, tn, W_A), F32),
        compiler_params=pltpu.CompilerParams(
            dimension_semantics=("arbitrary",), vmem_limit_bytes=VMEM_LIMIT_BYTES),
        name="dsa_sample",
    )(page_table, qa3, iq3, ikw3, ka3, va3, cik_t, ck_t, cv_t)


def _pack_w_in(w_in):
    split = SEG_IKW[0] + IDX_DIM + N_IDX_HEADS
    pad = jnp.zeros((D_MODEL, _W_IK_PAD), w_in.dtype)
    return jnp.concatenate([w_in[:, :split], pad, w_in[:, split:]], axis=1).astype(BF16)


def _pick_tile(n, pref):
    t = min(n, pref)
    while n % t:
        t //= 2
    return t


def _group(x, pos, dsa_fn, mem_k3, mem_v3, shift_prev, wkv_prev, t_valid, wts):
    b, t, _ = x.shape
    n = b * t
    x2 = x.reshape(n, D_MODEL)
    tm = _pick_tile(n, 256)
    qa, ka, va, iq, ikw, pb, cq, gs = _inproj(x2, pos, wts["norm_mix_g"], wts["w_in"], wts["b_gate"], tm)
    r3 = lambda a: a.reshape(b, t, a.shape[-1])
    oa = dsa_fn(r3(qa), r3(iq), r3(ikw), r3(ka), r3(va))
    chunk = _pick_tile(t, RWKV_CHUNK)
    ob, wkv_new = _rwkv(r3(pb), shift_prev, wkv_prev, wts["rwkv"], chunk, t_valid)
    oc = _xattn(r3(cq), mem_k3, mem_v3, _pick_tile(t, 512))
    y = _merge_mlp(x2, oa.reshape(n, W_A), ob.reshape(n, W_B), oc.reshape(n, W_C), gs,
                   wts["w_branch_a"], wts["w_branch_b"], wts["w_branch_c"], wts["w_out"],
                   wts["norm_mlp_g"], wts["w_mlp_up"], wts["w_mlp_down"], wts["norm_final_g"], tm)
    return y.reshape(b, t, D_MODEL), r3(ka), r3(va), r3(ikw)[..., :IDX_DIM], wkv_new, r3(pb)


def kernel(x_prompt, x_sample, mem_prompt, cache_k, cache_v, cache_idx_k, cache_mem_k, cache_mem_v,
           state_wkv, state_shift, page_table, norm_mix_g, norm_mem_g, norm_mlp_g, norm_final_g,
           w_in, b_gate, w_mem_kv, rwkv_mu, rwkv_w0, rwkv_w_up, rwkv_a0, rwkv_a_up, rwkv_g_up,
           rwkv_k_k, rwkv_k_a, rwkv_r_k, rwkv_ln_g, rwkv_ln_b, w_branch_a, w_branch_b, w_branch_c,
           w_out, w_mlp_up, w_mlp_down):
    bf = lambda w: w.astype(BF16)
    wts = dict(
        norm_mix_g=norm_mix_g, norm_mlp_g=norm_mlp_g, norm_final_g=norm_final_g,
        w_in=_pack_w_in(w_in), b_gate=b_gate,
        rwkv=dict(mu=rwkv_mu, w0=rwkv_w0, w_up=rwkv_w_up, a0=rwkv_a0, a_up=rwkv_a_up, g_up=rwkv_g_up,
                  k_k=rwkv_k_k, k_a=rwkv_k_a, r_k=rwkv_r_k, ln_g=rwkv_ln_g, ln_b=rwkv_ln_b),
        w_branch_a=bf(w_branch_a), w_branch_b=bf(w_branch_b), w_branch_c=bf(w_branch_c),
        w_out=bf(w_out), w_mlp_up=bf(w_mlp_up), w_mlp_down=bf(w_mlp_down))

    b, t, _ = x_prompt.shape
    n_mem = mem_prompt.shape[1]
    mem2 = mem_prompt.reshape(b * n_mem, D_MODEL)
    mk, mv = _memkv(mem2, norm_mem_g, bf(w_mem_kv), _pick_tile(b * n_mem, 256))
    mk3, mv3 = mk.reshape(b, n_mem, W_C), mv.reshape(b, n_mem, W_C)
    shift0 = jnp.zeros((b, RWKV_PROJ), F32)
    wkv0 = jnp.zeros((b, N_HEADS_B, HEAD_DIM_B, HEAD_DIM_B), F32)
    dsa_p = functools.partial(_dsa_prompt, tq=_pick_tile(t, 256))
    y_p, k_p, v_p, ik_p, wkv_p, pb_p = _group(
        x_prompt, jnp.arange(t), dsa_p, mk3, mv3, shift0, wkv0, None, wts)

    db, tn, _ = x_sample.shape
    n_pages = page_table.shape[1]
    page = cache_idx_k.shape[1]
    past = n_pages * page
    tp = -(-tn // SUBLANES) * SUBLANES
    x_s = jnp.pad(x_sample, ((0, 0), (0, tp - tn), (0, 0)))
    dsa_s = functools.partial(_dsa_sample, cache_idx_k=cache_idx_k, cache_k=cache_k, cache_v=cache_v,
                              page_table=page_table, t_new=tn)
    y_s, k_s, v_s, ik_s, wkv_s, pb_s = _group(
        x_s, past + jnp.arange(tp), dsa_s, cache_mem_k.reshape(db, n_mem, W_C),
        cache_mem_v.reshape(db, n_mem, W_C), state_shift, state_wkv, tn, wts)

    heads = lambda a: a.reshape(a.shape[0], a.shape[1], N_KV_A, HEAD_DIM_A)
    memh = lambda a: a.reshape(b, n_mem, N_HEADS_C, HEAD_DIM_C)
    return (y_p, y_s[:, :tn], heads(k_p), heads(v_p), ik_p, memh(mk3), memh(mv3), wkv_p, pb_p[:, -1],
            heads(k_s[:, :tn]), heads(v_s[:, :tn]), ik_s[:, :tn], wkv_s, pb_s[:, tn - 1])
```

```python
import functools
import math

import jax
import jax.numpy as jnp
from jax import lax
from jax.experimental import pallas as pl
from jax.experimental.pallas import tpu as pltpu

F32 = jnp.float32
BF16 = jnp.bfloat16

D_MODEL = 1024
N_HEADS_A, N_KV_A, HEAD_DIM_A = 8, 2, 64
W_A = N_HEADS_A * HEAD_DIM_A
W_KV = N_KV_A * HEAD_DIM_A
N_IDX_HEADS, IDX_DIM = 4, 64
W_IQ = N_IDX_HEADS * IDX_DIM
TOPK_MAX = 256
ROPE_THETA = 500000.0
ROT_DIM = 16
N_HEADS_B, HEAD_DIM_B = 8, 64
W_B = N_HEADS_B * HEAD_DIM_B
D_DECAY_LORA, D_AAA_LORA, D_GATE_LORA = 64, 64, 128
RWKV_PROJ = 3 * W_B + D_DECAY_LORA + D_AAA_LORA + D_GATE_LORA
GN_EPS = 64e-5
N_HEADS_C, HEAD_DIM_C = 4, 128
W_C = N_HEADS_C * HEAD_DIM_C
N_BRANCH = 3
D_FF = 4 * D_MODEL
RMS_EPS = 1e-6

LANES = 128
SUBLANES = 8
VMEM_LIMIT_BYTES = 56 * 1024 * 1024

INT_MIN = -(2 ** 31)
I16_MIN = -(2 ** 15)

RWKV_CHUNK = 128

_W_IK_PAD = LANES - IDX_DIM - N_IDX_HEADS
SEG_QA = (0, W_A)
SEG_KA = (W_A, W_KV)
SEG_VA = (W_A + W_KV, W_KV)
SEG_IQ = (W_A + 2 * W_KV, W_IQ)
SEG_IKW = (SEG_IQ[0] + W_IQ, LANES)
SEG_PB = (SEG_IKW[0] + LANES, RWKV_PROJ)
SEG_CQ = (SEG_PB[0] + RWKV_PROJ, W_C)
SEG_G = (SEG_CQ[0] + W_C, N_BRANCH * D_MODEL)
W_PACKED = SEG_G[0] + SEG_G[1]


def _rmsnorm(x, g):
    ms = jnp.mean(x * x, axis=-1, keepdims=True)
    return x * lax.rsqrt(ms + RMS_EPS) * g


def _rope(y, cos, s_lo, s_hi):
    return y * cos + pltpu.roll(y, LANES - ROT_DIM // 2, axis=1) * s_lo + pltpu.roll(y, ROT_DIM // 2, axis=1) * s_hi


def _inproj_kernel(x_ref, g_ref, w_ref, bg_ref, cos_ref, slo_ref, shi_ref,
                   qa_ref, ka_ref, va_ref, iq_ref, ikw_ref, pb_ref, cq_ref, gs_ref):
    xn = _rmsnorm(x_ref[...], g_ref[...]).astype(BF16)
    cos, s_lo, s_hi = cos_ref[...], slo_ref[...], shi_ref[...]

    def proj(off, width):
        return jnp.dot(xn, w_ref[:, off:off + width], preferred_element_type=F32)

    def roped(seg, out_ref):
        for c in range(0, seg[1], LANES):
            out_ref[:, c:c + LANES] = _rope(proj(seg[0] + c, LANES), cos, s_lo, s_hi)

    roped(SEG_QA, qa_ref)
    roped(SEG_KA, ka_ref)
    va_ref[...] = proj(*SEG_VA)
    roped(SEG_IQ, iq_ref)
    y = proj(*SEG_IKW)
    lane = lax.broadcasted_iota(jnp.int32, y.shape, 1)
    ikw_ref[...] = jnp.where(lane < IDX_DIM, _rope(y, cos, s_lo, s_hi), y * (W_IQ ** -0.5))
    for c in range(0, SEG_PB[1], 256):
        pb_ref[:, c:c + 256] = proj(SEG_PB[0] + c, 256)
    cq_ref[...] = proj(*SEG_CQ)
    for c in range(0, SEG_G[1], 512):
        gs_ref[:, c:c + 512] = jax.nn.sigmoid(proj(SEG_G[0] + c, 512) + bg_ref[:, c:c + 512])


def _rope_tables(pos, rows):
    half = ROT_DIM // 2
    inv_freq = ROPE_THETA ** (-jnp.arange(half, dtype=F32) / half)
    ang = pos.astype(F32)[:, None] * inv_freq[None, :]
    cos, sin = jnp.cos(ang), jnp.sin(ang)
    t = pos.shape[0]
    ones = jnp.ones((t, HEAD_DIM_A - ROT_DIM), F32)
    zeros_h = jnp.zeros((t, half), F32)
    zeros_r = jnp.zeros((t, HEAD_DIM_A - ROT_DIM), F32)
    c64 = jnp.concatenate([cos, cos, ones], axis=1)
    lo64 = jnp.concatenate([-sin, zeros_h, zeros_r], axis=1)
    hi64 = jnp.concatenate([zeros_h, sin, zeros_r], axis=1)
    reps = rows // t
    return tuple(jnp.tile(jnp.concatenate([a, a], axis=1), (reps, 1)) for a in (c64, lo64, hi64))


def _inproj(x2, pos, norm_g, w_packed, b_gate, tm):
    n = x2.shape[0]
    t = pos.shape[0]
    rows = max(t, tm)
    cos, s_lo, s_hi = _rope_tables(pos, rows)
    nt = rows // tm
    row = lambda i: (i, 0)
    const = lambda i: (0, 0)
    tab = pl.BlockSpec((tm, LANES), lambda i: (i % nt, 0))
    widths = [SEG_QA[1], SEG_KA[1], SEG_VA[1], SEG_IQ[1], SEG_IKW[1], SEG_PB[1], SEG_CQ[1], SEG_G[1]]
    return pl.pallas_call(
        _inproj_kernel,
        grid=(n // tm,),
        in_specs=[
            pl.BlockSpec((tm, D_MODEL), row),
            pl.BlockSpec((1, D_MODEL), const),
            pl.BlockSpec((D_MODEL, W_PACKED), const, pipeline_mode=pl.Buffered(1)),
            pl.BlockSpec((1, SEG_G[1]), const),
            tab, tab, tab,
        ],
        out_specs=[pl.BlockSpec((tm, w), row) for w in widths],
        out_shape=[jax.ShapeDtypeStruct((n, w), F32) for w in widths],
        compiler_params=pltpu.CompilerParams(
            dimension_semantics=("arbitrary",), vmem_limit_bytes=VMEM_LIMIT_BYTES),
        name="inproj",
    )(x2, norm_g.reshape(1, D_MODEL), w_packed, b_gate.reshape(1, -1), cos, s_lo, s_hi)


def _memkv_kernel(x_ref, g_ref, w_ref, k_ref, v_ref):
    xn = _rmsnorm(x_ref[...], g_ref[...]).astype(BF16)
    k_ref[...] = jnp.dot(xn, w_ref[:, :W_C], preferred_element_type=F32)
    v_ref[...] = jnp.dot(xn, w_ref[:, W_C:], preferred_element_type=F32)


def _memkv(mem2, norm_g, w_bf16, tm):
    n = mem2.shape[0]
    row = lambda i: (i, 0)
    const = lambda i: (0, 0)
    return pl.pallas_call(
        _memkv_kernel,
        grid=(n // tm,),
        in_specs=[pl.BlockSpec((tm, D_MODEL), row), pl.BlockSpec((1, D_MODEL), const),
                  pl.BlockSpec((D_MODEL, 2 * W_C), const)],
        out_specs=[pl.BlockSpec((tm, W_C), row)] * 2,
        out_shape=[jax.ShapeDtypeStruct((n, W_C), F32)] * 2,
        compiler_params=pltpu.CompilerParams(dimension_semantics=("arbitrary",)),
        name="memkv",
    )(mem2, norm_g.reshape(1, D_MODEL), w_bf16)


def _xattn_kernel(q_ref, k_ref, v_ref, o_ref):
    units = [(g, slice(h * HEAD_DIM_C, (h + 1) * HEAD_DIM_C))
             for g in range(q_ref.shape[0]) for h in range(N_HEADS_C)]
    s = [lax.dot_general(q_ref[g, :, sl].astype(BF16), k_ref[g, :, sl].astype(BF16), (((1,), (1,)), ((), ())),
                         preferred_element_type=F32) * (HEAD_DIM_C ** -0.5) for g, sl in units]
    p = []
    for sj in s:
        e = jnp.exp(sj - jnp.max(sj, axis=-1, keepdims=True))
        p.append((e / jnp.sum(e, axis=-1, keepdims=True)).astype(BF16))
    for pj, (g, sl) in zip(p, units):
        o_ref[g, :, sl] = jnp.dot(pj, v_ref[g, :, sl].astype(BF16), preferred_element_type=F32)


def _xattn(cq3, mk3, mv3, tq, nb):
    b, t, _ = cq3.shape
    n_mem = mk3.shape[1]
    return pl.pallas_call(
        _xattn_kernel,
        grid=(b // nb, t // tq),
        in_specs=[pl.BlockSpec((nb, tq, W_C), lambda i, j: (i, j, 0)),
                  pl.BlockSpec((nb, n_mem, W_C), lambda i, j: (i, 0, 0)),
                  pl.BlockSpec((nb, n_mem, W_C), lambda i, j: (i, 0, 0))],
        out_specs=pl.BlockSpec((nb, tq, W_C), lambda i, j: (i, j, 0)),
        out_shape=jax.ShapeDtypeStruct((b, t, W_C), F32),
        compiler_params=pltpu.CompilerParams(dimension_semantics=("arbitrary", "arbitrary")),
        name="xattn",
    )(cq3, mk3, mv3)


def _merge_mlp_kernel(x_ref, oa_ref, ob_ref, oc_ref, gs_ref, wa_ref, wb_ref, wc_ref, wo_ref,
                      gm_ref, wu_ref, wd_ref, gf_ref, y_ref):
    def bdot(a, w):
        return jnp.dot(a.astype(BF16), w, preferred_element_type=F32)

    merged = (gs_ref[:, :D_MODEL] * bdot(oa_ref[...], wa_ref[...])
              + gs_ref[:, D_MODEL:2 * D_MODEL] * bdot(ob_ref[...], wb_ref[...])
              + gs_ref[:, 2 * D_MODEL:] * bdot(oc_ref[...], wc_ref[...]))
    h = x_ref[...] + bdot(merged, wo_ref[...])
    hn = _rmsnorm(h, gm_ref[...]).astype(BF16)
    acc = h
    for c in range(0, D_FF, 1024):
        u = jnp.dot(hn, wu_ref[:, c:c + 1024], preferred_element_type=F32)
        acc = acc + bdot(jnp.square(jnp.maximum(u, 0.0)), wd_ref[c:c + 1024, :])
    y_ref[...] = _rmsnorm(acc, gf_ref[...])


def _merge_mlp(x2, oa, ob, oc, gs, wa, wb, wc, wo, g_mlp, wu, wd, g_final, tm):
    n = x2.shape[0]
    row = lambda i: (i, 0)
    const = lambda i: (0, 0)
    resident = lambda shape: pl.BlockSpec(shape, const, pipeline_mode=pl.Buffered(1))
    return pl.pallas_call(
        _merge_mlp_kernel,
        grid=(n // tm,),
        in_specs=[
            pl.BlockSpec((tm, D_MODEL), row),
            pl.BlockSpec((tm, W_A), row), pl.BlockSpec((tm, W_B), row), pl.BlockSpec((tm, W_C), row),
            pl.BlockSpec((tm, N_BRANCH * D_MODEL), row),
            resident((W_A, D_MODEL)), resident((W_B, D_MODEL)), resident((W_C, D_MODEL)),
            resident((D_MODEL, D_MODEL)),
            pl.BlockSpec((1, D_MODEL), const),
            resident((D_MODEL, D_FF)), resident((D_FF, D_MODEL)),
            pl.BlockSpec((1, D_MODEL), const),
        ],
        out_specs=pl.BlockSpec((tm, D_MODEL), row),
        out_shape=jax.ShapeDtypeStruct((n, D_MODEL), F32),
        compiler_params=pltpu.CompilerParams(
            dimension_semantics=("arbitrary",), vmem_limit_bytes=VMEM_LIMIT_BYTES),
        name="merge_mlp",
    )(x2, oa, ob, oc, gs, wa, wb, wc, wo, g_mlp.reshape(1, -1), wu, wd, g_final.reshape(1, -1))


def _bf16_terms(x, n):
    terms = []
    for _ in range(n):
        t = x.astype(BF16)
        terms.append(t)
        x = x - t.astype(F32)
    return terms


def _dot_split(a, b_terms):
    (ah, al), (bh, bl) = _bf16_terms(a, 2), b_terms
    d = lambda x, y: jnp.dot(x, y, preferred_element_type=F32)
    return d(ah, bh) + (d(ah, bl) + d(al, bh))


def _head_sums(x):
    lo = lax.broadcasted_iota(jnp.int32, (x.shape[0], LANES), 1) < HEAD_DIM_B
    out = []
    for c in range(0, x.shape[1], LANES):
        xc = x[:, c:c + LANES]
        s_lo = jnp.sum(jnp.where(lo, xc, 0.0), axis=-1, keepdims=True)
        s_hi = jnp.sum(jnp.where(lo, 0.0, xc), axis=-1, keepdims=True)
        out.append(jnp.where(lo, s_lo, s_hi))
    return jnp.concatenate(out, axis=1)


def _rwkv_kernel(pb_ref, sh_ref, s0_ref, mu_ref, w0_ref, wup_ref, a0_ref, aup_ref, gup_ref,
                 kk_ref, ka_ref, rk_ref, lng_ref, lnb_ref, ob_ref, sout_ref, carry_ref, state_ref,
                 *, chunk, t_valid):
    c = pl.program_id(1)
    nb = pb_ref.shape[0]

    @pl.when(c == 0)
    def _():
        carry_ref[...] = sh_ref[...]
        state_ref[...] = s0_ref[...]

    row1 = lax.broadcasted_iota(jnp.int32, (chunk, 1), 0)
    rr = lax.broadcasted_iota(jnp.int32, (chunk, chunk), 0)
    cc = lax.broadcasted_iota(jnp.int32, (chunk, chunk), 1)
    strict, incl = rr > cc, rr >= cc
    tri = incl.astype(BF16)
    w_up, a_up, g_up = (_bf16_terms(w[...], 2) for w in (wup_ref, aup_ref, gup_ref))

    def prepare(bi):
        pbc = pb_ref[bi]
        prev = jnp.where(row1 == 0, carry_ref[bi], pltpu.roll(pbc, 1, axis=0))
        carry_ref[bi] = pbc[chunk - 1:chunk, :]
        ps = pbc + (prev - pbc) * mu_ref[...]
        r, k, v = ps[:, :W_B], ps[:, W_B:2 * W_B], ps[:, 2 * W_B:3 * W_B]
        o = 3 * W_B
        wl = ps[:, o:o + D_DECAY_LORA]
        al = ps[:, o + D_DECAY_LORA:o + D_DECAY_LORA + D_AAA_LORA]
        gl = ps[:, o + D_DECAY_LORA + D_AAA_LORA:]
        z = -(w0_ref[...] + _dot_split(jnp.tanh(wl), w_up))
        softplus = jnp.maximum(z, 0.0) + jnp.log(1.0 + jnp.exp(-jnp.abs(z)))
        ld = -jnp.exp(-softplus - 0.5)
        alpha = jax.nn.sigmoid(a0_ref[...] + _dot_split(al, a_up))
        gate = _dot_split(jax.nn.sigmoid(gl), g_up)
        kkf = k * kk_ref[...]
        khf = k * (1.0 + (alpha - 1.0) * ka_ref[...])
        if t_valid is not None:
            valid = (row1 + c * chunk) < t_valid
            ld = jnp.where(valid, ld, 0.0)
            alpha = jnp.where(valid, alpha, 0.0)
            khf = jnp.where(valid, khf, 0.0)
        cum = sum(jnp.dot(tri, part, preferred_element_type=F32) for part in _bf16_terms(ld, 3))
        ecum, einv, eprev = jnp.exp(cum), jnp.exp(-cum), jnp.exp(cum - ld)
        kkn = kkf * lax.rsqrt(jnp.maximum(_head_sums(kkf * kkf), 1e-24))
        b_t = kkn * alpha * einv
        k_t = khf * einv
        g_c = ecum[chunk - 1:chunk, :]
        return dict(r=r, v=v, khf=khf, gate=gate, g_c=g_c,
                    a_t=(-kkn * eprev).astype(BF16), r_t=(r * ecum).astype(BF16),
                    b_c=(b_t * g_c).astype(BF16), k_c=(k_t * g_c).astype(BF16),
                    b_t=b_t.astype(BF16), k_t=k_t.astype(BF16), vb=v.astype(BF16))

    def mm(a, b):
        return jnp.dot(a.astype(BF16), b.astype(BF16), preferred_element_type=F32)

    def mm_t(a, b):
        return lax.dot_general(a.astype(BF16), b.astype(BF16), (((1,), (1,)), ((), ())),
                               preferred_element_type=F32)

    rows = [prepare(bi) for bi in range(nb)]
    n_it = max(1, int(math.ceil(math.log2(chunk))))
    units = [(bi, h, slice(h * HEAD_DIM_B, (h + 1) * HEAD_DIM_B))
             for bi in range(nb) for h in range(N_HEADS_B)]
    idx = range(len(units))
    col = lambda name: [rows[bi][name][:, sl] for bi, _, sl in units]
    a_t, r_t, b_t, k_t, b_c, k_c, vb = (col(n) for n in ("a_t", "r_t", "b_t", "k_t", "b_c", "k_c", "vb"))
    x1 = [jnp.concatenate([a_t[j], r_t[j]], axis=0) for j in idx]
    s0 = [state_ref[bi, h] for bi, h, _ in units]
    p0 = [mm_t(x1[j], s0[j]) for j in idx]
    if chunk % LANES == 0:
        g = [mm_t(x1[j], jnp.concatenate([b_t[j], k_t[j]], axis=0)) for j in idx]
        incl2 = jnp.concatenate([incl, incl], axis=1)
        u = [p0[j][:chunk] + mm(jnp.where(strict, g[j][:chunk, chunk:], 0.0), vb[j]) for j in idx]
        lp = [jnp.where(strict, g[j][:chunk, :chunk], 0.0).astype(BF16) for j in idx]
        for it in range(n_it - 1):
            sq = [mm(lp[j], jnp.concatenate([lp[j], u[j].astype(BF16)], axis=1)) for j in idx]
            u = [u[j] + sq[j][:, chunk:] for j in idx]
            lp = [sq[j][:, :chunk].astype(BF16) for j in idx]
        u = [u[j] + mm(lp[j], u[j]) for j in idx]
        uv = [jnp.concatenate([u[j].astype(BF16), vb[j]], axis=0) for j in idx]
        y = [p0[j][chunk:] + mm(jnp.where(incl2, g[j][chunk:], 0.0), uv[j]) for j in idx]
    else:
        g_b = [mm_t(x1[j], b_t[j]) for j in idx]
        g_k = [mm_t(x1[j], k_t[j]) for j in idx]
        u = [p0[j][:chunk] + mm(jnp.where(strict, g_k[j][:chunk], 0.0), vb[j]) for j in idx]
        lp = [jnp.where(strict, g_b[j][:chunk], 0.0).astype(BF16) for j in idx]
        for it in range(n_it):
            u = [u[j] + mm(lp[j], u[j]) for j in idx]
            if it + 1 < n_it:
                lp = [mm(lp[j], lp[j]).astype(BF16) for j in idx]
        y = [p0[j][chunk:] + mm(jnp.where(incl, g_b[j][chunk:], 0.0), u[j])
             + mm(jnp.where(incl, g_k[j][chunk:], 0.0), vb[j]) for j in idx]
        uv = [jnp.concatenate([u[j].astype(BF16), vb[j]], axis=0) for j in idx]
    for j, (bi, h, sl) in enumerate(units):
        bk = jnp.concatenate([b_c[j], k_c[j]], axis=0)
        state_ref[bi, h] = s0[j] * rows[bi]["g_c"][:, sl] + lax.dot_general(
            uv[j], bk, (((0,), (0,)), ((), ())), preferred_element_type=F32)

    for bi, row in enumerate(rows):
        y_all = jnp.concatenate(y[bi * N_HEADS_B:(bi + 1) * N_HEADS_B], axis=1)
        d = y_all - _head_sums(y_all) * (1.0 / HEAD_DIM_B)
        var = _head_sums(d * d) * (1.0 / HEAD_DIM_B)
        yn = d * lax.rsqrt(var + GN_EPS) * lng_ref[...] + lnb_ref[...]
        bonus = _head_sums(row["r"] * row["khf"] * rk_ref[...]) * row["v"]
        ob_ref[bi] = (yn + bonus) * row["gate"]

    @pl.when(c == pl.num_programs(1) - 1)
    def _():
        sout_ref[...] = state_ref[...]


def _rwkv(pb3, shift_prev, wkv_prev, p, chunk, nb, t_valid):
    b, t, _ = pb3.shape
    const = lambda i, j: (0, 0)
    vec = lambda n: pl.BlockSpec((1, n), const)
    state_spec = pl.BlockSpec((nb, N_HEADS_B, HEAD_DIM_B, HEAD_DIM_B), lambda i, j: (i, 0, 0, 0))
    return pl.pallas_call(
        functools.partial(_rwkv_kernel, chunk=chunk, t_valid=t_valid),
        grid=(b // nb, t // chunk),
        in_specs=[
            pl.BlockSpec((nb, chunk, RWKV_PROJ), lambda i, j: (i, j, 0)),
            pl.BlockSpec((nb, 1, RWKV_PROJ), lambda i, j: (i, 0, 0)),
            state_spec,
            vec(RWKV_PROJ), vec(W_B), pl.BlockSpec((D_DECAY_LORA, W_B), const),
            vec(W_B), pl.BlockSpec((D_AAA_LORA, W_B), const), pl.BlockSpec((D_GATE_LORA, W_B), const),
            vec(W_B), vec(W_B), vec(W_B), vec(W_B), vec(W_B),
        ],
        out_specs=[pl.BlockSpec((nb, chunk, W_B), lambda i, j: (i, j, 0)), state_spec],
        out_shape=[jax.ShapeDtypeStruct((b, t, W_B), F32),
                   jax.ShapeDtypeStruct(wkv_prev.shape, F32)],
        scratch_shapes=[pltpu.VMEM((nb, 1, RWKV_PROJ), F32),
                        pltpu.VMEM((nb, N_HEADS_B, HEAD_DIM_B, HEAD_DIM_B), F32)],
        compiler_params=pltpu.CompilerParams(dimension_semantics=("arbitrary", "arbitrary")),
        name="rwkv",
    )(pb3, shift_prev.reshape(b, 1, RWKV_PROJ), wkv_prev,
      p["mu"].reshape(1, -1), p["w0"].reshape(1, -1), p["w_up"], p["a0"].reshape(1, -1), p["a_up"],
      p["g_up"], p["k_k"].reshape(1, -1), p["k_a"].reshape(1, -1), p["r_k"].reshape(1, -1),
      p["ln_g"].reshape(1, -1), p["ln_b"].reshape(1, -1))


def _sort_key(s):
    b = lax.bitcast_convert_type(s + 0.0, jnp.int32)
    return b ^ (lax.shift_right_arithmetic(b, 31) & 0x7FFFFFFF)


def _kth_largest(count_ge, shape, topk):
    def body(it, t):
        cand = t + lax.shift_left(jnp.int32(1), 31 - it)
        return jnp.where(count_ge(cand) >= topk, cand, t)
    return lax.fori_loop(0, 32, body, jnp.full(shape, INT_MIN, jnp.int32))


def _tie_cutoff(count_eq_before, shape, need, n_bits):
    def body(it, x):
        cand = x + lax.shift_left(jnp.int32(1), n_bits - 1 - it)
        return jnp.where(count_eq_before(cand) < need, cand, x)
    return lax.fori_loop(0, n_bits, body, jnp.zeros(shape, jnp.int32))


def _dsa_prompt_kernel(qa_ref, iq_ref, ikwq_ref, ka_ref, va_ref, ikw_ref, o_ref,
                       key_ref, kh_ref, kl_ref, m_ref, l_ref, acc_ref, *, tq, tk, topk):
    i = pl.program_id(1)
    nkt = ((i + 1) * tq + tk - 1) // tk
    iq = iq_ref[0]
    iq_heads = jnp.concatenate(
        [iq[:, h * IDX_DIM:(h + 1) * IDX_DIM] for h in range(N_IDX_HEADS)], axis=0).astype(BF16)
    w_t = ikwq_ref[0].T
    kpos0 = lax.broadcasted_iota(jnp.int32, (tk, tq), 0)
    qpos = i * tq + lax.broadcasted_iota(jnp.int32, (tk, tq), 1)
    dn_t = (((1,), (1,)), ((), ()))
    dn_0 = (((0,), (0,)), ((), ()))

    def tile(kt):
        return pl.ds(pl.multiple_of(kt * tk, tk), tk)

    def scores(kt, carry):
        ik = ikw_ref[0, tile(kt), :][:, :IDX_DIM].astype(BF16)
        d = lax.dot_general(ik, iq_heads, dn_t, preferred_element_type=F32)
        s = jnp.zeros((tk, tq), F32)
        for h in range(N_IDX_HEADS):
            s = s + w_t[IDX_DIM + h:IDX_DIM + h + 1, :] * jnp.maximum(d[:, h * tq:(h + 1) * tq], 0.0)
        key = jnp.where(kpos0 + kt * tk <= qpos, _sort_key(s), INT_MIN)
        key_ref[tile(kt), :] = key
        kh_ref[tile(kt), :] = lax.shift_right_arithmetic(key, 16).astype(jnp.int16)
        kl_ref[tile(kt), :] = ((key & 0xFFFF) + I16_MIN).astype(jnp.int16)
        return carry

    lax.fori_loop(0, nkt, scores, 0)

    rows16 = 2 * SUBLANES

    def count16(ref, pred):
        def body(kt, acc):
            one = jnp.where(pred(ref[tile(kt), :]), jnp.int16(1), jnp.int16(0))
            for j in range(0, tk, rows16):
                acc = acc + one[j:j + rows16, :]
            return acc
        acc = lax.fori_loop(0, nkt, body, jnp.zeros((rows16, tq), jnp.int16))
        return jnp.sum(acc.astype(jnp.int32), axis=0, keepdims=True)

    def kth16(ref, above):
        def body(it, t):
            cand = t + lax.shift_left(jnp.int32(1), 15 - it)
            c16 = cand.astype(jnp.int16)
            return jnp.where(above + count16(ref, lambda blk: blk >= c16) >= topk, cand, t)
        return lax.fori_loop(0, 16, body, jnp.full((1, tq), I16_MIN, jnp.int32))

    t_hi = kth16(kh_ref, 0)
    h16 = t_hi.astype(jnp.int16)
    n_hi = count16(kh_ref, lambda blk: blk > h16)

    def keep_equal(kt, carry):
        kl_ref[tile(kt), :] = jnp.where(kh_ref[tile(kt), :] == h16, kl_ref[tile(kt), :], jnp.int16(I16_MIN))
        return carry

    lax.fori_loop(0, nkt, keep_equal, 0)
    t_lo = kth16(kl_ref, n_hi)
    l16 = t_lo.astype(jnp.int16)
    thr = lax.shift_left(t_hi, 16) + (t_lo - I16_MIN)
    live = thr != INT_MIN
    need = (topk - n_hi - count16(kl_ref, lambda blk: blk > l16)).astype(F32)
    before = (lax.broadcasted_iota(jnp.int32, (tk, tk), 0)
              > lax.broadcasted_iota(jnp.int32, (tk, tk), 1)).astype(BF16)
    ones = jnp.ones((tk, SUBLANES), BF16)

    def select(kt, tied_before):
        blk = key_ref[tile(kt), :]
        tied = (blk == thr) & live
        tied_b = jnp.where(tied, 1.0, 0.0).astype(BF16)
        rank = tied_before + jnp.dot(before, tied_b, preferred_element_type=F32)
        sel = (blk > thr) | (tied & (rank < need))
        key_ref[tile(kt), :] = lax.bitcast_convert_type(jnp.where(sel, 0.0, -jnp.inf), jnp.int32)
        return tied_before + lax.dot_general(ones, tied_b, dn_0, preferred_element_type=F32)[:1, :]

    lax.fori_loop(0, nkt, select, jnp.zeros((1, tq), F32))

    m_ref[...] = jnp.full(m_ref.shape, -jnp.inf, F32)
    l_ref[...] = jnp.zeros(l_ref.shape, F32)
    acc_ref[...] = jnp.zeros(acc_ref.shape, F32)
    qa = (qa_ref[0] * (HEAD_DIM_A ** -0.5 * math.log2(math.e))).astype(BF16)
    heads = range(N_HEADS_A)
    low_lanes = lax.broadcasted_iota(jnp.int32, (tk, W_KV), 1) < HEAD_DIM_A
    hs = [slice(h * HEAD_DIM_A, (h + 1) * HEAD_DIM_A) for h in heads]
    gs = [slice((h // (N_HEADS_A // N_KV_A)) * HEAD_DIM_A, (h // (N_HEADS_A // N_KV_A) + 1) * HEAD_DIM_A)
          for h in heads]

    def attend(kt, carry):
        bias = lax.bitcast_convert_type(key_ref[tile(kt), :], F32)
        kk = ka_ref[0, tile(kt), :].astype(BF16)
        v32 = va_ref[0, tile(kt), :]
        v_aug = [jnp.where(low_lanes, v32 if g == 0 else pltpu.roll(v32, HEAD_DIM_A, axis=1), 1.0).astype(BF16)
                 for g in range(N_KV_A)]
        s = [lax.dot_general(kk[:, gs[h]], qa[:, hs[h]], dn_t, preferred_element_type=F32) + bias
             for h in heads]
        for h in heads:
            m_old = m_ref[h:h + 1, :]
            m_new = jnp.maximum(m_old, jnp.max(s[h], axis=0, keepdims=True))
            m_safe = jnp.where(m_new == -jnp.inf, 0.0, m_new)
            alpha = jnp.exp2(m_old - m_safe)
            p = jnp.exp2((s[h] - m_safe).astype(BF16))
            m_ref[h:h + 1, :] = m_new
            pv = lax.dot_general(v_aug[h // (N_HEADS_A // N_KV_A)], p, dn_0, preferred_element_type=F32)
            l_ref[h:h + 1, :] = alpha * l_ref[h:h + 1, :] + pv[HEAD_DIM_A:HEAD_DIM_A + 1, :]
            acc_ref[hs[h], :] = alpha * acc_ref[hs[h], :] + pv[:HEAD_DIM_A, :]
        return carry

    lax.fori_loop(0, nkt, attend, 0)
    for h in heads:
        acc_ref[hs[h], :] = acc_ref[hs[h], :] / l_ref[h:h + 1, :]
    o_ref[0] = acc_ref[...].T


def _dsa_prompt(qa3, iq3, ikw3, ka3, va3, tq):
    b, t, _ = qa3.shape
    topk = min(TOPK_MAX, t // 4)
    assert tq >= topk and t % tq == 0
    tk = 2 * tq if t % (2 * tq) == 0 else tq
    qtile = lambda w: pl.BlockSpec((1, tq, w), lambda i, j: (i, j, 0))
    whole = lambda w: pl.BlockSpec((1, t, w), lambda i, j: (i, 0, 0))
    return pl.pallas_call(
        functools.partial(_dsa_prompt_kernel, tq=tq, tk=tk, topk=topk),
        grid=(b, t // tq),
        in_specs=[qtile(W_A), qtile(W_IQ), qtile(LANES), whole(W_KV), whole(W_KV), whole(LANES)],
        out_specs=qtile(W_A),
        out_shape=jax.ShapeDtypeStruct((b, t, W_A), F32),
        scratch_shapes=[pltpu.VMEM((t, tq), jnp.int32),
                        pltpu.VMEM((t, tq), jnp.int16), pltpu.VMEM((t, tq), jnp.int16),
                        pltpu.VMEM((N_HEADS_A, tq), F32), pltpu.VMEM((N_HEADS_A, tq), F32),
                        pltpu.VMEM((W_A, tq), F32)],
        compiler_params=pltpu.CompilerParams(
            dimension_semantics=("arbitrary", "arbitrary"), vmem_limit_bytes=VMEM_LIMIT_BYTES),
        name="dsa_prompt",
    )(qa3, iq3, ikw3, ka3, va3, ikw3)


def _dsa_sample_kernel(pt_ref, qa_ref, iq_ref, ikwn_ref, kan_ref, van_ref, cik_ref, ck_ref, cv_ref,
                       o_ref, ik_buf, k_buf, v_buf, sems, *, n_pages, page, tn, topk, n_bits):
    b = pl.program_id(0)
    nb = pl.num_programs(0)
    past = n_pages * page
    total = past + LANES
    slot = b % 2

    def page_copies(row, dst, j):
        pg = pt_ref[row, j]
        cols = pl.ds(j * page, page)
        return (pltpu.make_async_copy(cik_ref.at[pg], ik_buf.at[dst, :, cols], sems.at[dst, 0]),
                pltpu.make_async_copy(ck_ref.at[pg], k_buf.at[dst, :, cols], sems.at[dst, 1]),
                pltpu.make_async_copy(cv_ref.at[pg], v_buf.at[dst, :, cols], sems.at[dst, 2]))

    def start_row(row, dst):
        for j in range(n_pages):
            for cp in page_copies(row, dst, j):
                cp.start()

    @pl.when(b == 0)
    def _():
        start_row(0, 0)

    @pl.when(b + 1 < nb)
    def _():
        start_row(b + 1, 1 - slot)

    for j in range(n_pages):
        for cp in page_copies(b, slot, j):
            cp.wait()

    def tail_tile(x):
        return jnp.concatenate([x, jnp.zeros((LANES - tn, x.shape[1]), F32)], axis=0).T

    ikw = ikwn_ref[0]
    ik_buf[slot, :, past:] = tail_tile(ikw)[:IDX_DIM, :]
    k_buf[slot, :, past:] = tail_tile(kan_ref[0])
    v_buf[slot, :, past:] = tail_tile(van_ref[0])

    iq = iq_ref[0]
    iq_heads = jnp.concatenate(
        [iq[:, h * IDX_DIM:(h + 1) * IDX_DIM] for h in range(N_IDX_HEADS)], axis=0).astype(BF16)
    d = jnp.dot(iq_heads, ik_buf[slot].astype(BF16), preferred_element_type=F32)
    s = jnp.zeros((tn, total), F32)
    for h in range(N_IDX_HEADS):
        s = s + ikw[:, IDX_DIM + h:IDX_DIM + h + 1] * jnp.maximum(d[h * tn:(h + 1) * tn, :], 0.0)
    kpos = lax.broadcasted_iota(jnp.int32, (tn, total), 1)
    qpos = past + lax.broadcasted_iota(jnp.int32, (tn, total), 0)
    key = jnp.where(kpos <= qpos, _sort_key(s), INT_MIN)

    def count(m):
        return jnp.sum(m.astype(jnp.int32), axis=1, keepdims=True)

    thr = _kth_largest(lambda c: count(key >= c), (tn, 1), topk)
    live = thr != INT_MIN
    need = topk - count(key > thr)
    cut = _tie_cutoff(lambda x: count((key == thr) & (kpos < x)), (tn, 1), need, n_bits)
    sel = (key > thr) | ((key == thr) & live & (kpos <= cut))
    bias = jnp.where(sel, 0.0, -jnp.inf)

    rep = N_HEADS_A // N_KV_A
    qa = (qa_ref[0] * (HEAD_DIM_A ** -0.5)).astype(BF16)
    bias_g = jnp.concatenate([bias] * rep, axis=0)
    for g in range(N_KV_A):
        gs = slice(g * HEAD_DIM_A, (g + 1) * HEAD_DIM_A)
        qg = jnp.concatenate(
            [qa[:, (g * rep + r) * HEAD_DIM_A:(g * rep + r + 1) * HEAD_DIM_A] for r in range(rep)], axis=0)
        sc = jnp.dot(qg, k_buf[slot, gs, :].astype(BF16), preferred_element_type=F32) + bias_g
        m = jnp.max(sc, axis=1, keepdims=True)
        p = jnp.exp(sc - m)
        og = lax.dot_general(p.astype(BF16), v_buf[slot, gs, :].astype(BF16), (((1,), (1,)), ((), ())),
                             preferred_element_type=F32) / jnp.sum(p, axis=1, keepdims=True)
        for r in range(rep):
            h = g * rep + r
            o_ref[0, :, h * HEAD_DIM_A:(h + 1) * HEAD_DIM_A] = og[r * tn:(r + 1) * tn, :]


def _dsa_sample(qa3, iq3, ikw3, ka3, va3, cache_idx_k, cache_k, cache_v, page_table, t_new):
    db, tn, _ = qa3.shape
    n_pages = page_table.shape[1]
    n_phys, page, _ = cache_idx_k.shape
    past = n_pages * page
    total = past + LANES
    topk = min(TOPK_MAX, (past + t_new) // 4)
    n_bits = max(1, int(math.ceil(math.log2(total))))
    cik_t = jnp.transpose(cache_idx_k, (0, 2, 1))
    ck_t = jnp.transpose(cache_k, (0, 2, 3, 1)).reshape(n_phys, W_KV, page)
    cv_t = jnp.transpose(cache_v, (0, 2, 3, 1)).reshape(n_phys, W_KV, page)
    new = lambda w: pl.BlockSpec((1, tn, w), lambda i, pt: (i, 0, 0))
    hbm = pl.BlockSpec(memory_space=pl.ANY)
    return pl.pallas_call(
        functools.partial(_dsa_sample_kernel, n_pages=n_pages, page=page, tn=tn, topk=topk, n_bits=n_bits),
        grid_spec=pltpu.PrefetchScalarGridSpec(
            num_scalar_prefetch=1,
            grid=(db,),
            in_specs=[new(W_A), new(W_IQ), new(LANES), new(W_KV), new(W_KV), hbm, hbm, hbm],
            out_specs=new(W_A),
            scratch_shapes=[pltpu.VMEM((2, IDX_DIM, total), F32), pltpu.VMEM((2, W_KV, total), F32),
                            pltpu.VMEM((2, W_KV, total), F32), pltpu.SemaphoreType.DMA((2, 3))],
        ),
        out_shape=jax.ShapeDtypeStruct((db, tn, W_A), F32),
        compiler_params=pltpu.CompilerParams(
            dimension_semantics=("arbitrary",), vmem_limit_bytes=VMEM_LIMIT_BYTES),
        name="dsa_sample",
    )(page_table, qa3, iq3, ikw3, ka3, va3, cik_t, ck_t, cv_t)


def _pack_w_in(w_in):
    split = SEG_IKW[0] + IDX_DIM + N_IDX_HEADS
    pad = jnp.zeros((D_MODEL, _W_IK_PAD), w_in.dtype)
    return jnp.concatenate([w_in[:, :split], pad, w_in[:, split:]], axis=1).astype(BF16)


def _pick_tile(n, pref):
    t = min(n, pref)
    while n % t:
        t //= 2
    return t


def _group(x, pos, dsa_fn, mem_k3, mem_v3, shift_prev, wkv_prev, t_valid, wts):
    b, t, _ = x.shape
    n = b * t
    x2 = x.reshape(n, D_MODEL)
    tm = _pick_tile(n, 256)
    qa, ka, va, iq, ikw, pb, cq, gs = _inproj(x2, pos, wts["norm_mix_g"], wts["w_in"], wts["b_gate"], tm)
    r3 = lambda a: a.reshape(b, t, a.shape[-1])
    oa = dsa_fn(r3(qa), r3(iq), r3(ikw), r3(ka), r3(va))
    chunk = _pick_tile(t, RWKV_CHUNK)
    nb = _pick_tile(b, max(2, 32 // chunk))
    ob, wkv_new = _rwkv(r3(pb), shift_prev, wkv_prev, wts["rwkv"], chunk, nb, t_valid)
    tq = _pick_tile(t, 512)
    oc = _xattn(r3(cq), mem_k3, mem_v3, tq, _pick_tile(b, max(1, 64 // tq)))
    y = _merge_mlp(x2, oa.reshape(n, W_A), ob.reshape(n, W_B), oc.reshape(n, W_C), gs,
                   wts["w_branch_a"], wts["w_branch_b"], wts["w_branch_c"], wts["w_out"],
                   wts["norm_mlp_g"], wts["w_mlp_up"], wts["w_mlp_down"], wts["norm_final_g"], tm)
    return y.reshape(b, t, D_MODEL), r3(ka), r3(va), r3(ikw)[..., :IDX_DIM], wkv_new, r3(pb)


def kernel(x_prompt, x_sample, mem_prompt, cache_k, cache_v, cache_idx_k, cache_mem_k, cache_mem_v,
           state_wkv, state_shift, page_table, norm_mix_g, norm_mem_g, norm_mlp_g, norm_final_g,
           w_in, b_gate, w_mem_kv, rwkv_mu, rwkv_w0, rwkv_w_up, rwkv_a0, rwkv_a_up, rwkv_g_up,
           rwkv_k_k, rwkv_k_a, rwkv_r_k, rwkv_ln_g, rwkv_ln_b, w_branch_a, w_branch_b, w_branch_c,
           w_out, w_mlp_up, w_mlp_down):
    bf = lambda w: w.astype(BF16)
    wts = dict(
        norm_mix_g=norm_mix_g, norm_mlp_g=norm_mlp_g, norm_final_g=norm_final_g,
        w_in=_pack_w_in(w_in), b_gate=b_gate,
        rwkv=dict(mu=rwkv_mu, w0=rwkv_w0, w_up=rwkv_w_up, a0=rwkv_a0, a_up=rwkv_a_up, g_up=rwkv_g_up,
                  k_k=rwkv_k_k, k_a=rwkv_k_a, r_k=rwkv_r_k, ln_g=rwkv_ln_g, ln_b=rwkv_ln_b),
        w_branch_a=bf(w_branch_a), w_branch_b=bf(w_branch_b), w_branch_c=bf(w_branch_c),
        w_out=bf(w_out), w_mlp_up=bf(w_mlp_up), w_mlp_down=bf(w_mlp_down))

    b, t, _ = x_prompt.shape
    n_mem = mem_prompt.shape[1]
    mem2 = mem_prompt.reshape(b * n_mem, D_MODEL)
    mk, mv = _memkv(mem2, norm_mem_g, bf(w_mem_kv), _pick_tile(b * n_mem, 256))
    mk3, mv3 = mk.reshape(b, n_mem, W_C), mv.reshape(b, n_mem, W_C)
    shift0 = jnp.zeros((b, RWKV_PROJ), F32)
    wkv0 = jnp.zeros((b, N_HEADS_B, HEAD_DIM_B, HEAD_DIM_B), F32)
    dsa_p = functools.partial(_dsa_prompt, tq=_pick_tile(t, 256))
    y_p, k_p, v_p, ik_p, wkv_p, pb_p = _group(
        x_prompt, jnp.arange(t), dsa_p, mk3, mv3, shift0, wkv0, None, wts)

    db, tn, _ = x_sample.shape
    n_pages = page_table.shape[1]
    page = cache_idx_k.shape[1]
    past = n_pages * page
    tp = -(-tn // SUBLANES) * SUBLANES
    x_s = jnp.pad(x_sample, ((0, 0), (0, tp - tn), (0, 0)))
    dsa_s = functools.partial(_dsa_sample, cache_idx_k=cache_idx_k, cache_k=cache_k, cache_v=cache_v,
                              page_table=page_table, t_new=tn)
    y_s, k_s, v_s, ik_s, wkv_s, pb_s = _group(
        x_s, past + jnp.arange(tp), dsa_s, cache_mem_k.reshape(db, n_mem, W_C),
        cache_mem_v.reshape(db, n_mem, W_C), state_shift, state_wkv, tn, wts)

    heads = lambda a: a.reshape(a.shape[0], a.shape[1], N_KV_A, HEAD_DIM_A)
    memh = lambda a: a.reshape(b, n_mem, N_HEADS_C, HEAD_DIM_C)
    return (y_p, y_s[:, :tn], heads(k_p), heads(v_p), ik_p, memh(mk3), memh(mv3), wkv_p, pb_p[:, -1],
            heads(k_s[:, :tn]), heads(v_s[:, :tn]), ik_s[:, :tn], wkv_s, pb_s[:, tn - 1])
```

```python
import functools
import math

import jax
import jax.numpy as jnp
from jax import lax
from jax.experimental import pallas as pl
from jax.experimental.pallas import tpu as pltpu

F32 = jnp.float32
BF16 = jnp.bfloat16

D_MODEL = 1024
N_HEADS_A, N_KV_A, HEAD_DIM_A = 8, 2, 64
W_A = N_HEADS_A * HEAD_DIM_A
W_KV = N_KV_A * HEAD_DIM_A
N_IDX_HEADS, IDX_DIM = 4, 64
W_IQ = N_IDX_HEADS * IDX_DIM
TOPK_MAX = 256
ROPE_THETA = 500000.0
ROT_DIM = 16
N_HEADS_B, HEAD_DIM_B = 8, 64
W_B = N_HEADS_B * HEAD_DIM_B
D_DECAY_LORA, D_AAA_LORA, D_GATE_LORA = 64, 64, 128
RWKV_PROJ = 3 * W_B + D_DECAY_LORA + D_AAA_LORA + D_GATE_LORA
GN_EPS = 64e-5
N_HEADS_C, HEAD_DIM_C = 4, 128
W_C = N_HEADS_C * HEAD_DIM_C
N_BRANCH = 3
D_FF = 4 * D_MODEL
RMS_EPS = 1e-6

LANES = 128
SUBLANES = 8
VMEM_LIMIT_BYTES = 56 * 1024 * 1024

INT_MIN = -(2 ** 31)
I16_MIN = -(2 ** 15)

RWKV_CHUNK = 128

_W_IK_PAD = LANES - IDX_DIM - N_IDX_HEADS
SEG_QA = (0, W_A)
SEG_KA = (W_A, W_KV)
SEG_VA = (W_A + W_KV, W_KV)
SEG_IQ = (W_A + 2 * W_KV, W_IQ)
SEG_IKW = (SEG_IQ[0] + W_IQ, LANES)
SEG_PB = (SEG_IKW[0] + LANES, RWKV_PROJ)
SEG_CQ = (SEG_PB[0] + RWKV_PROJ, W_C)
SEG_G = (SEG_CQ[0] + W_C, N_BRANCH * D_MODEL)
W_PACKED = SEG_G[0] + SEG_G[1]


def _rmsnorm(x, g):
    ms = jnp.mean(x * x, axis=-1, keepdims=True)
    return x * lax.rsqrt(ms + RMS_EPS) * g


def _rope(y, cos, s_lo, s_hi):
    return y * cos + pltpu.roll(y, LANES - ROT_DIM // 2, axis=1) * s_lo + pltpu.roll(y, ROT_DIM // 2, axis=1) * s_hi


def _inproj_kernel(x_ref, g_ref, w_ref, bg_ref, cos_ref, slo_ref, shi_ref,
                   qa_ref, ka_ref, va_ref, iq_ref, ikw_ref, pb_ref, cq_ref, gs_ref):
    xn = _rmsnorm(x_ref[...], g_ref[...]).astype(BF16)
    cos, s_lo, s_hi = cos_ref[...], slo_ref[...], shi_ref[...]

    def proj(off, width):
        return jnp.dot(xn, w_ref[:, off:off + width], preferred_element_type=F32)

    def rope_into(out_ref, c, y):
        for j in range(0, y.shape[1], LANES):
            out_ref[:, c + j:c + j + LANES] = _rope(y[:, j:j + LANES], cos, s_lo, s_hi)

    for c in range(0, SEG_QA[1], 2 * LANES):
        rope_into(qa_ref, c, proj(SEG_QA[0] + c, 2 * LANES))
    kv = proj(SEG_KA[0], SEG_KA[1] + SEG_VA[1])
    rope_into(ka_ref, 0, kv[:, :SEG_KA[1]])
    va_ref[...] = kv[:, SEG_KA[1]:]
    rope_into(iq_ref, 0, proj(*SEG_IQ))
    y = proj(*SEG_IKW)
    lane = lax.broadcasted_iota(jnp.int32, y.shape, 1)
    ikw_ref[...] = jnp.where(lane < IDX_DIM, _rope(y, cos, s_lo, s_hi), y * (W_IQ ** -0.5))
    for c in range(0, SEG_PB[1], 256):
        pb_ref[:, c:c + 256] = proj(SEG_PB[0] + c, 256)
    cq_ref[...] = proj(*SEG_CQ)
    for c in range(0, SEG_G[1], 512):
        gs_ref[:, c:c + 512] = jax.nn.sigmoid(proj(SEG_G[0] + c, 512) + bg_ref[:, c:c + 512])


def _rope_tables(pos, rows):
    half = ROT_DIM // 2
    inv_freq = ROPE_THETA ** (-jnp.arange(half, dtype=F32) / half)
    ang = pos.astype(F32)[:, None] * inv_freq[None, :]
    cos, sin = jnp.cos(ang), jnp.sin(ang)
    t = pos.shape[0]
    ones = jnp.ones((t, HEAD_DIM_A - ROT_DIM), F32)
    zeros_h = jnp.zeros((t, half), F32)
    zeros_r = jnp.zeros((t, HEAD_DIM_A - ROT_DIM), F32)
    c64 = jnp.concatenate([cos, cos, ones], axis=1)
    lo64 = jnp.concatenate([-sin, zeros_h, zeros_r], axis=1)
    hi64 = jnp.concatenate([zeros_h, sin, zeros_r], axis=1)
    reps = rows // t
    return tuple(jnp.tile(jnp.concatenate([a, a], axis=1), (reps, 1)) for a in (c64, lo64, hi64))


def _inproj(x2, pos, norm_g, w_packed, b_gate, tm):
    n = x2.shape[0]
    t = pos.shape[0]
    rows = max(t, tm)
    cos, s_lo, s_hi = _rope_tables(pos, rows)
    nt = rows // tm
    row = lambda i: (i, 0)
    const = lambda i: (0, 0)
    tab = pl.BlockSpec((tm, LANES), lambda i: (i % nt, 0))
    widths = [SEG_QA[1], SEG_KA[1], SEG_VA[1], SEG_IQ[1], SEG_IKW[1], SEG_PB[1], SEG_CQ[1], SEG_G[1]]
    return pl.pallas_call(
        _inproj_kernel,
        grid=(n // tm,),
        in_specs=[
            pl.BlockSpec((tm, D_MODEL), row),
            pl.BlockSpec((1, D_MODEL), const),
            pl.BlockSpec((D_MODEL, W_PACKED), const, pipeline_mode=pl.Buffered(1)),
            pl.BlockSpec((1, SEG_G[1]), const),
            tab, tab, tab,
        ],
        out_specs=[pl.BlockSpec((tm, w), row) for w in widths],
        out_shape=[jax.ShapeDtypeStruct((n, w), F32) for w in widths],
        compiler_params=pltpu.CompilerParams(
            dimension_semantics=("arbitrary",), vmem_limit_bytes=VMEM_LIMIT_BYTES),
        name="inproj",
    )(x2, norm_g.reshape(1, D_MODEL), w_packed, b_gate.reshape(1, -1), cos, s_lo, s_hi)


def _memkv_kernel(x_ref, g_ref, w_ref, k_ref, v_ref):
    xn = _rmsnorm(x_ref[...], g_ref[...]).astype(BF16)
    tm = x_ref.shape[0]
    for out_ref, off in ((k_ref, 0), (v_ref, W_C)):
        y = jnp.dot(xn, w_ref[:, off:off + W_C], preferred_element_type=F32)
        for h in range(N_HEADS_C):
            out_ref[pl.ds(h, tm, stride=N_HEADS_C), :] = y[:, h * HEAD_DIM_C:(h + 1) * HEAD_DIM_C]


def _memkv(mem2, norm_g, w_bf16, tm):
    n = mem2.shape[0]
    row = lambda i: (i, 0)
    const = lambda i: (0, 0)
    return pl.pallas_call(
        _memkv_kernel,
        grid=(n // tm,),
        in_specs=[pl.BlockSpec((tm, D_MODEL), row), pl.BlockSpec((1, D_MODEL), const),
                  pl.BlockSpec((D_MODEL, 2 * W_C), const)],
        out_specs=[pl.BlockSpec((tm * N_HEADS_C, HEAD_DIM_C), row)] * 2,
        out_shape=[jax.ShapeDtypeStruct((n * N_HEADS_C, HEAD_DIM_C), F32)] * 2,
        compiler_params=pltpu.CompilerParams(dimension_semantics=("arbitrary",)),
        name="memkv",
    )(mem2, norm_g.reshape(1, D_MODEL), w_bf16)


def _xattn_kernel(q_ref, k_ref, v_ref, o_ref):
    n_mem = k_ref.shape[1] // N_HEADS_C
    units = [(g, h) for g in range(q_ref.shape[0]) for h in range(N_HEADS_C)]
    head = lambda ref, g, h: ref[g, pl.ds(h, n_mem, stride=N_HEADS_C), :].astype(BF16)
    cols = lambda h: slice(h * HEAD_DIM_C, (h + 1) * HEAD_DIM_C)
    s = [lax.dot_general(q_ref[g, :, cols(h)].astype(BF16), head(k_ref, g, h), (((1,), (1,)), ((), ())),
                         preferred_element_type=F32) * (HEAD_DIM_C ** -0.5) for g, h in units]
    p = []
    for sj in s:
        e = jnp.exp(sj - jnp.max(sj, axis=-1, keepdims=True))
        p.append((e / jnp.sum(e, axis=-1, keepdims=True)).astype(BF16))
    for pj, (g, h) in zip(p, units):
        o_ref[g, :, cols(h)] = jnp.dot(pj, head(v_ref, g, h), preferred_element_type=F32)


def _xattn(cq3, mk3, mv3, tq, nb):
    b, t, _ = cq3.shape
    rows = mk3.shape[1]
    return pl.pallas_call(
        _xattn_kernel,
        grid=(b // nb, t // tq),
        in_specs=[pl.BlockSpec((nb, tq, W_C), lambda i, j: (i, j, 0)),
                  pl.BlockSpec((nb, rows, HEAD_DIM_C), lambda i, j: (i, 0, 0)),
                  pl.BlockSpec((nb, rows, HEAD_DIM_C), lambda i, j: (i, 0, 0))],
        out_specs=pl.BlockSpec((nb, tq, W_C), lambda i, j: (i, j, 0)),
        out_shape=jax.ShapeDtypeStruct((b, t, W_C), F32),
        compiler_params=pltpu.CompilerParams(dimension_semantics=("arbitrary", "arbitrary")),
        name="xattn",
    )(cq3, mk3, mv3)


def _merge_mlp_kernel(x_ref, oa_ref, ob_ref, oc_ref, gs_ref, wa_ref, wb_ref, wc_ref, wo_ref,
                      gm_ref, wu_ref, wd_ref, gf_ref, y_ref):
    def bdot(a, w):
        return jnp.dot(a.astype(BF16), w, preferred_element_type=F32)

    merged = (gs_ref[:, :D_MODEL] * bdot(oa_ref[...], wa_ref[...])
              + gs_ref[:, D_MODEL:2 * D_MODEL] * bdot(ob_ref[...], wb_ref[...])
              + gs_ref[:, 2 * D_MODEL:] * bdot(oc_ref[...], wc_ref[...]))
    h = x_ref[...] + bdot(merged, wo_ref[...])
    hn = _rmsnorm(h, gm_ref[...]).astype(BF16)
    acc = h
    for c in range(0, D_FF, 1024):
        u = jnp.dot(hn, wu_ref[:, c:c + 1024], preferred_element_type=F32)
        acc = acc + bdot(jnp.square(jnp.maximum(u, 0.0)), wd_ref[c:c + 1024, :])
    y_ref[...] = _rmsnorm(acc, gf_ref[...])


def _merge_mlp(x2, oa, ob, oc, gs, wa, wb, wc, wo, g_mlp, wu, wd, g_final, tm):
    n = x2.shape[0]
    row = lambda i: (i, 0)
    const = lambda i: (0, 0)
    resident = lambda shape: pl.BlockSpec(shape, const, pipeline_mode=pl.Buffered(1))
    return pl.pallas_call(
        _merge_mlp_kernel,
        grid=(n // tm,),
        in_specs=[
            pl.BlockSpec((tm, D_MODEL), row),
            pl.BlockSpec((tm, W_A), row), pl.BlockSpec((tm, W_B), row), pl.BlockSpec((tm, W_C), row),
            pl.BlockSpec((tm, N_BRANCH * D_MODEL), row),
            resident((W_A, D_MODEL)), resident((W_B, D_MODEL)), resident((W_C, D_MODEL)),
            resident((D_MODEL, D_MODEL)),
            pl.BlockSpec((1, D_MODEL), const),
            resident((D_MODEL, D_FF)), resident((D_FF, D_MODEL)),
            pl.BlockSpec((1, D_MODEL), const),
        ],
        out_specs=pl.BlockSpec((tm, D_MODEL), row),
        out_shape=jax.ShapeDtypeStruct((n, D_MODEL), F32),
        compiler_params=pltpu.CompilerParams(
            dimension_semantics=("arbitrary",), vmem_limit_bytes=VMEM_LIMIT_BYTES),
        name="merge_mlp",
    )(x2, oa, ob, oc, gs, wa, wb, wc, wo, g_mlp.reshape(1, -1), wu, wd, g_final.reshape(1, -1))


def _bf16_terms(x, n):
    terms = []
    for _ in range(n):
        t = x.astype(BF16)
        terms.append(t)
        x = x - t.astype(F32)
    return terms


def _dot_split(a, b_terms):
    (ah, al), (bh, bl) = _bf16_terms(a, 2), b_terms
    d = lambda x, y: jnp.dot(x, y, preferred_element_type=F32)
    return d(ah, bh) + (d(ah, bl) + d(al, bh))


def _head_sums(x):
    lo = lax.broadcasted_iota(jnp.int32, (x.shape[0], LANES), 1) < HEAD_DIM_B
    out = []
    for c in range(0, x.shape[1], LANES):
        xc = x[:, c:c + LANES]
        s_lo = jnp.sum(jnp.where(lo, xc, 0.0), axis=-1, keepdims=True)
        s_hi = jnp.sum(jnp.where(lo, 0.0, xc), axis=-1, keepdims=True)
        out.append(jnp.where(lo, s_lo, s_hi))
    return jnp.concatenate(out, axis=1)


def _rwkv_kernel(pb_ref, sh_ref, s0_ref, mu_ref, w0_ref, wup_ref, a0_ref, aup_ref, gup_ref,
                 kk_ref, ka_ref, rk_ref, lng_ref, lnb_ref, ob_ref, sout_ref, carry_ref, state_ref,
                 *, chunk, t_valid):
    c = pl.program_id(1)
    nb = pb_ref.shape[0]

    @pl.when(c == 0)
    def _():
        carry_ref[...] = sh_ref[...]
        state_ref[...] = s0_ref[...]

    row1 = lax.broadcasted_iota(jnp.int32, (chunk, 1), 0)
    rr = lax.broadcasted_iota(jnp.int32, (chunk, chunk), 0)
    cc = lax.broadcasted_iota(jnp.int32, (chunk, chunk), 1)
    strict, incl = rr > cc, rr >= cc
    tri = incl.astype(BF16)
    w_up, a_up, g_up = (_bf16_terms(w[...], 2) for w in (wup_ref, aup_ref, gup_ref))

    def prepare(bi):
        pbc = pb_ref[bi]
        prev = jnp.where(row1 == 0, carry_ref[bi], pltpu.roll(pbc, 1, axis=0))
        carry_ref[bi] = pbc[chunk - 1:chunk, :]
        ps = pbc + (prev - pbc) * mu_ref[...]
        r, k, v = ps[:, :W_B], ps[:, W_B:2 * W_B], ps[:, 2 * W_B:3 * W_B]
        o = 3 * W_B
        wl = ps[:, o:o + D_DECAY_LORA]
        al = ps[:, o + D_DECAY_LORA:o + D_DECAY_LORA + D_AAA_LORA]
        gl = ps[:, o + D_DECAY_LORA + D_AAA_LORA:]
        z = -(w0_ref[...] + _dot_split(jnp.tanh(wl), w_up))
        softplus = jnp.maximum(z, 0.0) + jnp.log(1.0 + jnp.exp(-jnp.abs(z)))
        ld = -jnp.exp(-softplus - 0.5)
        alpha = jax.nn.sigmoid(a0_ref[...] + _dot_split(al, a_up))
        gate = _dot_split(jax.nn.sigmoid(gl), g_up)
        kkf = k * kk_ref[...]
        khf = k * (1.0 + (alpha - 1.0) * ka_ref[...])
        if t_valid is not None:
            valid = (row1 + c * chunk) < t_valid
            ld = jnp.where(valid, ld, 0.0)
            alpha = jnp.where(valid, alpha, 0.0)
            khf = jnp.where(valid, khf, 0.0)
        cum = sum(jnp.dot(tri, part, preferred_element_type=F32) for part in _bf16_terms(ld, 3))
        ecum, einv, eprev = jnp.exp(cum), jnp.exp(-cum), jnp.exp(cum - ld)
        kkn = kkf * lax.rsqrt(jnp.maximum(_head_sums(kkf * kkf), 1e-24))
        b_t = kkn * alpha * einv
        k_t = khf * einv
        g_c = ecum[chunk - 1:chunk, :]
        return dict(r=r, v=v, khf=khf, gate=gate, g_c=g_c,
                    a_t=(-kkn * eprev).astype(BF16), r_t=(r * ecum).astype(BF16),
                    b_c=(b_t * g_c).astype(BF16), k_c=(k_t * g_c).astype(BF16),
                    b_t=b_t.astype(BF16), k_t=k_t.astype(BF16), vb=v.astype(BF16))

    def mm(a, b):
        return jnp.dot(a.astype(BF16), b.astype(BF16), preferred_element_type=F32)

    def mm_t(a, b):
        return lax.dot_general(a.astype(BF16), b.astype(BF16), (((1,), (1,)), ((), ())),
                               preferred_element_type=F32)

    rows = [prepare(bi) for bi in range(nb)]
    n_it = max(1, int(math.ceil(math.log2(chunk))))
    units = [(bi, h, slice(h * HEAD_DIM_B, (h + 1) * HEAD_DIM_B))
             for bi in range(nb) for h in range(N_HEADS_B)]
    idx = range(len(units))
    col = lambda name: [rows[bi][name][:, sl] for bi, _, sl in units]
    a_t, r_t, b_t, k_t, b_c, k_c, vb = (col(n) for n in ("a_t", "r_t", "b_t", "k_t", "b_c", "k_c", "vb"))
    x1 = [jnp.concatenate([a_t[j], r_t[j]], axis=0) for j in idx]
    s0 = [state_ref[bi, h] for bi, h, _ in units]
    p0 = [mm_t(x1[j], s0[j]) for j in idx]
    if chunk % LANES == 0:
        g = [mm_t(x1[j], jnp.concatenate([b_t[j], k_t[j]], axis=0)) for j in idx]
        incl2 = jnp.concatenate([incl, incl], axis=1)
        u = [p0[j][:chunk] + mm(jnp.where(strict, g[j][:chunk, chunk:], 0.0), vb[j]) for j in idx]
        lp = [jnp.where(strict, g[j][:chunk, :chunk], 0.0).astype(BF16) for j in idx]
        for it in range(n_it - 1):
            sq = [mm(lp[j], jnp.concatenate([lp[j], u[j].astype(BF16)], axis=1)) for j in idx]
            u = [u[j] + sq[j][:, chunk:] for j in idx]
            lp = [sq[j][:, :chunk].astype(BF16) for j in idx]
        u = [u[j] + mm(lp[j], u[j]) for j in idx]
        uv = [jnp.concatenate([u[j].astype(BF16), vb[j]], axis=0) for j in idx]
        y = [p0[j][chunk:] + mm(jnp.where(incl2, g[j][chunk:], 0.0), uv[j]) for j in idx]
    else:
        g_b = [mm_t(x1[j], b_t[j]) for j in idx]
        g_k = [mm_t(x1[j], k_t[j]) for j in idx]
        u = [p0[j][:chunk] + mm(jnp.where(strict, g_k[j][:chunk], 0.0), vb[j]) for j in idx]
        lp = [jnp.where(strict, g_b[j][:chunk], 0.0).astype(BF16) for j in idx]
        for it in range(n_it):
            u = [u[j] + mm(lp[j], u[j]) for j in idx]
            if it + 1 < n_it:
                lp = [mm(lp[j], lp[j]).astype(BF16) for j in idx]
        y = [p0[j][chunk:] + mm(jnp.where(incl, g_b[j][chunk:], 0.0), u[j])
             + mm(jnp.where(incl, g_k[j][chunk:], 0.0), vb[j]) for j in idx]
        uv = [jnp.concatenate([u[j].astype(BF16), vb[j]], axis=0) for j in idx]
    for j, (bi, h, sl) in enumerate(units):
        bk = jnp.concatenate([b_c[j], k_c[j]], axis=0)
        state_ref[bi, h] = s0[j] * rows[bi]["g_c"][:, sl] + lax.dot_general(
            uv[j], bk, (((0,), (0,)), ((), ())), preferred_element_type=F32)

    for bi, row in enumerate(rows):
        y_all = jnp.concatenate(y[bi * N_HEADS_B:(bi + 1) * N_HEADS_B], axis=1)
        d = y_all - _head_sums(y_all) * (1.0 / HEAD_DIM_B)
        var = _head_sums(d * d) * (1.0 / HEAD_DIM_B)
        yn = d * lax.rsqrt(var + GN_EPS) * lng_ref[...] + lnb_ref[...]
        bonus = _head_sums(row["r"] * row["khf"] * rk_ref[...]) * row["v"]
        ob_ref[bi] = (yn + bonus) * row["gate"]

    @pl.when(c == pl.num_programs(1) - 1)
    def _():
        sout_ref[...] = state_ref[...]


def _rwkv(pb3, shift_prev, wkv_prev, p, chunk, nb, t_valid):
    b, t, _ = pb3.shape
    const = lambda i, j: (0, 0)
    vec = lambda n: pl.BlockSpec((1, n), const)
    state_spec = pl.BlockSpec((nb, N_HEADS_B, HEAD_DIM_B, HEAD_DIM_B), lambda i, j: (i, 0, 0, 0))
    return pl.pallas_call(
        functools.partial(_rwkv_kernel, chunk=chunk, t_valid=t_valid),
        grid=(b // nb, t // chunk),
        in_specs=[
            pl.BlockSpec((nb, chunk, RWKV_PROJ), lambda i, j: (i, j, 0)),
            pl.BlockSpec((nb, 1, RWKV_PROJ), lambda i, j: (i, 0, 0)),
            state_spec,
            vec(RWKV_PROJ), vec(W_B), pl.BlockSpec((D_DECAY_LORA, W_B), const),
            vec(W_B), pl.BlockSpec((D_AAA_LORA, W_B), const), pl.BlockSpec((D_GATE_LORA, W_B), const),
            vec(W_B), vec(W_B), vec(W_B), vec(W_B), vec(W_B),
        ],
        out_specs=[pl.BlockSpec((nb, chunk, W_B), lambda i, j: (i, j, 0)), state_spec],
        out_shape=[jax.ShapeDtypeStruct((b, t, W_B), F32),
                   jax.ShapeDtypeStruct(wkv_prev.shape, F32)],
        scratch_shapes=[pltpu.VMEM((nb, 1, RWKV_PROJ), F32),
                        pltpu.VMEM((nb, N_HEADS_B, HEAD_DIM_B, HEAD_DIM_B), F32)],
        compiler_params=pltpu.CompilerParams(dimension_semantics=("arbitrary", "arbitrary")),
        name="rwkv",
    )(pb3, shift_prev.reshape(b, 1, RWKV_PROJ), wkv_prev,
      p["mu"].reshape(1, -1), p["w0"].reshape(1, -1), p["w_up"], p["a0"].reshape(1, -1), p["a_up"],
      p["g_up"], p["k_k"].reshape(1, -1), p["k_a"].reshape(1, -1), p["r_k"].reshape(1, -1),
      p["ln_g"].reshape(1, -1), p["ln_b"].reshape(1, -1))


def _key_to_f32(k):
    return lax.bitcast_convert_type(k ^ (lax.shift_right_arithmetic(k, 31) & 0x7FFFFFFF), F32)


def _kth_largest(count_ge, shape, topk):
    def body(it, t):
        cand = t + lax.shift_left(jnp.int32(1), 31 - it)
        return jnp.where(count_ge(_key_to_f32(cand)) >= topk, cand, t)
    return _key_to_f32(lax.fori_loop(0, 32, body, jnp.full(shape, INT_MIN, jnp.int32)))


def _tie_cutoff(count_eq_before, shape, need, n_bits):
    def body(it, x):
        cand = x + lax.shift_left(jnp.int32(1), n_bits - 1 - it)
        return jnp.where(count_eq_before(cand) < need, cand, x)
    return lax.fori_loop(0, n_bits, body, jnp.zeros(shape, jnp.int32))


def _dsa_prompt_kernel(qa_ref, iq_ref, ikwq_ref, ka_ref, va_ref, ikw_ref, o_ref,
                       sc_ref, sb_ref, cnt_ref, cnt16_ref, tied_ref, m_ref, l_ref, acc_ref, *, tq, tk, topk):
    i = pl.program_id(1)
    n_full = ((i + 1) * tq) // tk
    has_part = ((i + 1) * tq) % tk != 0
    iq = iq_ref[0]
    iq_heads = jnp.concatenate(
        [iq[:, h * IDX_DIM:(h + 1) * IDX_DIM] for h in range(N_IDX_HEADS)], axis=0).astype(BF16)
    w_t = ikwq_ref[0].T
    sizes = sorted({tk, tq})
    kpos0 = {n: lax.broadcasted_iota(jnp.int32, (n, tq), 0) for n in sizes}
    qpos = {n: i * tq + lax.broadcasted_iota(jnp.int32, (n, tq), 1) for n in sizes}
    dn_t = (((1,), (1,)), ((), ()))
    dn_0 = (((0,), (0,)), ((), ()))

    def for_tiles(body, init, carry_ref=None):
        def full(kt, c):
            return body(pl.ds(pl.multiple_of(kt * tk, tk), tk), tk, c)
        c = lax.fori_loop(0, n_full, full, init)
        if tk == tq:
            return c
        last = pl.ds(pl.multiple_of(n_full * tk, tq), tq)
        if carry_ref is None:
            @pl.when(has_part)
            def _():
                body(last, tq, c)
            return c
        carry_ref[...] = c

        @pl.when(has_part)
        def _():
            carry_ref[...] = body(last, tq, carry_ref[...])
        return carry_ref[...]

    def scores(rows, n, carry):
        ik = ikw_ref[0, rows, :][:, :IDX_DIM].astype(BF16)
        d = lax.dot_general(ik, iq_heads, dn_t, preferred_element_type=F32)
        s = jnp.zeros((n, tq), F32)
        for h in range(N_IDX_HEADS):
            s = s + w_t[IDX_DIM + h:IDX_DIM + h + 1, :] * jnp.maximum(d[:, h * tq:(h + 1) * tq], 0.0)
        sc = jnp.where(kpos0[n] + rows.start <= qpos[n], s + 0.0, -jnp.inf)
        sc_ref[rows, :] = sc
        sb_ref[rows, :] = sc.astype(BF16)
        return carry

    for_tiles(scores, 0)

    def count(pred):
        def body(rows, n, acc):
            m = pred(sc_ref[rows, :]).astype(jnp.int32)
            return acc + jnp.sum(m.reshape(n // SUBLANES, SUBLANES, tq), axis=0)
        acc = for_tiles(body, jnp.zeros((SUBLANES, tq), jnp.int32), cnt_ref)
        return jnp.sum(acc, axis=0, keepdims=True)

    rows16 = 2 * SUBLANES

    def count_b(cand):
        def body(rows, n, acc):
            one = jnp.where(sb_ref[rows, :] >= cand, jnp.int16(1), jnp.int16(0))
            part = one[0:rows16, :]
            for j in range(rows16, n, rows16):
                part = part + one[j:j + rows16, :]
            return acc + part.astype(jnp.int32)
        acc = for_tiles(body, jnp.zeros((rows16, tq), jnp.int32), cnt16_ref)
        return jnp.sum(acc, axis=0, keepdims=True)

    def hi_step(it, t):
        cand = t + lax.shift_left(jnp.int32(1), 15 - it)
        c_b = _key_to_f32(lax.shift_left(cand, 16)).astype(BF16)
        return jnp.where(count_b(c_b) >= topk, cand, t)
    k_b = lax.shift_left(lax.fori_loop(0, 16, hi_step, jnp.full((1, tq), I16_MIN, jnp.int32)), 16)
    half = 1 << 15

    def lo_step(it, c):
        lo, hi = c
        mid = lo + lax.shift_right_arithmetic(hi - lo, 1)
        ok = count(lambda blk: blk >= _key_to_f32(mid)) >= topk
        return jnp.where(ok, mid, lo), jnp.where(ok, hi, mid)
    k_lo, _ = lax.fori_loop(0, 18, lo_step, (k_b - half - 1, k_b + 2 * half + 1))
    live = i * tq + lax.broadcasted_iota(jnp.int32, (1, tq), 1) + 1 >= topk
    thr = jnp.where(live, _key_to_f32(k_lo), -jnp.inf)
    need = (topk - count(lambda blk: blk > thr)).astype(F32)
    before = {n: (lax.broadcasted_iota(jnp.int32, (n, n), 0)
                  > lax.broadcasted_iota(jnp.int32, (n, n), 1)).astype(BF16) for n in sizes}
    ones = {n: jnp.ones((n, SUBLANES), BF16) for n in sizes}

    def select(rows, n, tied_before):
        blk = sc_ref[rows, :]
        tied = (blk == thr) & live
        tied_b = jnp.where(tied, 1.0, 0.0).astype(BF16)
        rank = tied_before + jnp.dot(before[n], tied_b, preferred_element_type=F32)
        sel = (blk > thr) | (tied & (rank < need))
        sc_ref[rows, :] = jnp.where(sel, 0.0, -jnp.inf)
        return tied_before + lax.dot_general(ones[n], tied_b, dn_0, preferred_element_type=F32)[:1, :]

    for_tiles(select, jnp.zeros((1, tq), F32), tied_ref)

    m_ref[...] = jnp.full(m_ref.shape, -jnp.inf, F32)
    l_ref[...] = jnp.zeros(l_ref.shape, F32)
    acc_ref[...] = jnp.zeros(acc_ref.shape, F32)
    qa = (qa_ref[0] * (HEAD_DIM_A ** -0.5 * math.log2(math.e))).astype(BF16)
    heads = range(N_HEADS_A)
    low_lanes = {n: lax.broadcasted_iota(jnp.int32, (n, W_KV), 1) < HEAD_DIM_A for n in sizes}
    hs = [slice(h * HEAD_DIM_A, (h + 1) * HEAD_DIM_A) for h in heads]
    gs = [slice((h // (N_HEADS_A // N_KV_A)) * HEAD_DIM_A, (h // (N_HEADS_A // N_KV_A) + 1) * HEAD_DIM_A)
          for h in heads]

    def attend(rows, n, carry):
        bias = sc_ref[rows, :]
        kk = ka_ref[0, rows, :].astype(BF16)
        v32 = va_ref[0, rows, :]
        v_aug = [jnp.where(low_lanes[n], v32 if g == 0 else pltpu.roll(v32, HEAD_DIM_A, axis=1), 1.0)
                 .astype(BF16) for g in range(N_KV_A)]
        s = [lax.dot_general(kk[:, gs[h]], qa[:, hs[h]], dn_t, preferred_element_type=F32) + bias
             for h in heads]
        for h in heads:
            m_old = m_ref[h:h + 1, :]
            m_new = jnp.maximum(m_old, jnp.max(s[h], axis=0, keepdims=True))
            m_safe = jnp.where(m_new == -jnp.inf, 0.0, m_new)
            alpha = jnp.exp2(m_old - m_safe)
            p = jnp.exp2((s[h] - m_safe).astype(BF16))
            m_ref[h:h + 1, :] = m_new
            pv = lax.dot_general(v_aug[h // (N_HEADS_A // N_KV_A)], p, dn_0, preferred_element_type=F32)
            l_ref[h:h + 1, :] = alpha * l_ref[h:h + 1, :] + pv[HEAD_DIM_A:HEAD_DIM_A + 1, :]
            acc_ref[hs[h], :] = alpha * acc_ref[hs[h], :] + pv[:HEAD_DIM_A, :]
        return carry

    for_tiles(attend, 0)
    for h in heads:
        acc_ref[hs[h], :] = acc_ref[hs[h], :] / l_ref[h:h + 1, :]
    o_ref[0] = acc_ref[...].T


def _dsa_prompt(qa3, iq3, ikw3, ka3, va3, tq):
    b, t, _ = qa3.shape
    topk = min(TOPK_MAX, t // 4)
    assert tq >= topk and t % tq == 0
    tk = 2 * tq if t % (2 * tq) == 0 else tq
    qtile = lambda w: pl.BlockSpec((1, tq, w), lambda i, j: (i, j, 0))
    whole = lambda w: pl.BlockSpec((1, t, w), lambda i, j: (i, 0, 0))
    return pl.pallas_call(
        functools.partial(_dsa_prompt_kernel, tq=tq, tk=tk, topk=topk),
        grid=(b, t // tq),
        in_specs=[qtile(W_A), qtile(W_IQ), qtile(LANES), whole(W_KV), whole(W_KV), whole(LANES)],
        out_specs=qtile(W_A),
        out_shape=jax.ShapeDtypeStruct((b, t, W_A), F32),
        scratch_shapes=[pltpu.VMEM((t, tq), F32), pltpu.VMEM((t, tq), BF16),
                        pltpu.VMEM((SUBLANES, tq), jnp.int32), pltpu.VMEM((2 * SUBLANES, tq), jnp.int32),
                        pltpu.VMEM((1, tq), F32),
                        pltpu.VMEM((N_HEADS_A, tq), F32), pltpu.VMEM((N_HEADS_A, tq), F32),
                        pltpu.VMEM((W_A, tq), F32)],
        compiler_params=pltpu.CompilerParams(
            dimension_semantics=("arbitrary", "arbitrary"), vmem_limit_bytes=VMEM_LIMIT_BYTES),
        name="dsa_prompt",
    )(qa3, iq3, ikw3, ka3, va3, ikw3)


def _dsa_sample_kernel(pt_ref, qa_ref, iq_ref, ikwn_ref, kan_ref, van_ref, cik_ref, ck_ref, cv_ref,
                       o_ref, ik_buf, k_buf, v_buf, sems, *, n_pages, page, tn, t_new, topk, n_bits):
    step = pl.program_id(0)
    n_steps = pl.num_programs(0)
    nr = qa_ref.shape[0]
    past = n_pages * page
    total = past + LANES
    slot = step % 2
    sem_k, sem_v = 2, 3

    def ik_copy(row, dst, r, j):
        return pltpu.make_async_copy(cik_ref.at[pt_ref[row, j]],
                                     ik_buf.at[dst, r, :, pl.ds(j * page, page)], sems.at[dst])

    def kv_copies(row, r, j):
        pg, cols = pt_ref[row, j], pl.ds(j * page, page)
        return (pltpu.make_async_copy(ck_ref.at[pg], k_buf.at[r, :, cols], sems.at[sem_k]),
                pltpu.make_async_copy(cv_ref.at[pg], v_buf.at[r, :, cols], sems.at[sem_v]))

    def start_ik(at_step, dst):
        for r in range(nr):
            for j in range(n_pages):
                ik_copy(at_step * nr + r, dst, r, j).start()

    @pl.when(step == 0)
    def _():
        start_ik(0, 0)

    for r in range(nr):
        for j in range(n_pages):
            for cp in kv_copies(step * nr + r, r, j):
                cp.start()

    @pl.when(step + 1 < n_steps)
    def _():
        start_ik(step + 1, 1 - slot)

    for r in range(nr):
        for j in range(n_pages):
            ik_copy(step * nr + r, slot, r, j).wait()

    def tail_tile(x):
        return jnp.concatenate([x, jnp.zeros((LANES - tn, x.shape[1]), F32)], axis=0).T

    rows_all = nr * tn
    keys = []
    for r in range(nr):
        ikw = ikwn_ref[r]
        ik_buf[slot, r, :, past:] = tail_tile(ikw)[:IDX_DIM, :]
        iq = iq_ref[r]
        iq_heads = jnp.concatenate(
            [iq[:, h * IDX_DIM:(h + 1) * IDX_DIM] for h in range(N_IDX_HEADS)], axis=0).astype(BF16)
        d = jnp.dot(iq_heads, ik_buf[slot, r].astype(BF16), preferred_element_type=F32)
        s = jnp.zeros((tn, total), F32)
        for h in range(N_IDX_HEADS):
            s = s + ikw[:, IDX_DIM + h:IDX_DIM + h + 1] * jnp.maximum(d[h * tn:(h + 1) * tn, :], 0.0)
        keys.append(s + 0.0)
    kpos = lax.broadcasted_iota(jnp.int32, (rows_all, total), 1)
    tok = lax.broadcasted_iota(jnp.int32, (rows_all, total), 0) % tn
    key = jnp.where(kpos <= past + tok, jnp.concatenate(keys, axis=0), -jnp.inf)

    def count(m):
        return jnp.sum(m.astype(jnp.int32), axis=1, keepdims=True)

    live = past + lax.broadcasted_iota(jnp.int32, (rows_all, 1), 0) % tn + 1 >= topk
    thr = jnp.where(live, _kth_largest(lambda c: count(key >= c), (rows_all, 1), topk), -jnp.inf)
    n_gt = count(key > thr)
    tied = (key == thr) & live
    real = tok[:, :LANES] < t_new
    surplus = jnp.sum((real & (n_gt + count(tied) > topk)).astype(jnp.int32))
    cut = lax.cond(
        surplus > 0,
        lambda: _tie_cutoff(lambda x: count(tied & (kpos < x)), (rows_all, 1), topk - n_gt, n_bits),
        lambda: jnp.full((rows_all, 1), 2 ** 31 - 1, jnp.int32))
    bias = jnp.where((key > thr) | (tied & (kpos <= cut)), 0.0, -jnp.inf)

    for r in range(nr):
        for j in range(n_pages):
            for cp in kv_copies(step * nr + r, r, j):
                cp.wait()

    rep = N_HEADS_A // N_KV_A
    for r in range(nr):
        k_buf[r, :, past:] = tail_tile(kan_ref[r])
        v_buf[r, :, past:] = tail_tile(van_ref[r])
        qa = (qa_ref[r] * (HEAD_DIM_A ** -0.5)).astype(BF16)
        bias_g = jnp.concatenate([bias[r * tn:(r + 1) * tn]] * rep, axis=0)
        for g in range(N_KV_A):
            gs = slice(g * HEAD_DIM_A, (g + 1) * HEAD_DIM_A)
            qg = jnp.concatenate(
                [qa[:, (g * rep + i) * HEAD_DIM_A:(g * rep + i + 1) * HEAD_DIM_A] for i in range(rep)], axis=0)
            sc = jnp.dot(qg, k_buf[r, gs, :].astype(BF16), preferred_element_type=F32) + bias_g
            m = jnp.max(sc, axis=1, keepdims=True)
            p = jnp.exp(sc - m)
            og = lax.dot_general(p.astype(BF16), v_buf[r, gs, :].astype(BF16), (((1,), (1,)), ((), ())),
                                 preferred_element_type=F32) / jnp.sum(p, axis=1, keepdims=True)
            for i in range(rep):
                h = g * rep + i
                o_ref[r, :, h * HEAD_DIM_A:(h + 1) * HEAD_DIM_A] = og[i * tn:(i + 1) * tn, :]


def _dsa_sample(qa3, iq3, ikw3, ka3, va3, cache_idx_k, cache_k, cache_v, page_table, t_new):
    db, tn, _ = qa3.shape
    n_pages = page_table.shape[1]
    n_phys, page, _ = cache_idx_k.shape
    past = n_pages * page
    total = past + LANES
    topk = min(TOPK_MAX, (past + t_new) // 4)
    n_bits = max(1, int(math.ceil(math.log2(total))))
    nr = _pick_tile(db, 2)
    cik_t = jnp.transpose(cache_idx_k, (0, 2, 1))
    ck_t = jnp.transpose(cache_k, (0, 2, 3, 1)).reshape(n_phys, W_KV, page)
    cv_t = jnp.transpose(cache_v, (0, 2, 3, 1)).reshape(n_phys, W_KV, page)
    new = lambda w: pl.BlockSpec((nr, tn, w), lambda i, pt: (i, 0, 0))
    hbm = pl.BlockSpec(memory_space=pl.ANY)
    return pl.pallas_call(
        functools.partial(_dsa_sample_kernel, n_pages=n_pages, page=page, tn=tn, t_new=t_new, topk=topk,
                          n_bits=n_bits),
        grid_spec=pltpu.PrefetchScalarGridSpec(
            num_scalar_prefetch=1,
            grid=(db // nr,),
            in_specs=[new(W_A), new(W_IQ), new(LANES), new(W_KV), new(W_KV), hbm, hbm, hbm],
            out_specs=new(W_A),
            scratch_shapes=[pltpu.VMEM((2, nr, IDX_DIM, total), F32), pltpu.VMEM((nr, W_KV, total), F32),
                            pltpu.VMEM((nr, W_KV, total), F32), pltpu.SemaphoreType.DMA((4,))],
        ),
        out_shape=jax.ShapeDtypeStruct((db, tn, W_A), F32),
        compiler_params=pltpu.CompilerParams(
            dimension_semantics=("arbitrary",), vmem_limit_bytes=VMEM_LIMIT_BYTES),
        name="dsa_sample",
    )(page_table, qa3, iq3, ikw3, ka3, va3, cik_t, ck_t, cv_t)


def _pack_w_in(w_in):
    split = SEG_IKW[0] + IDX_DIM + N_IDX_HEADS
    pad = jnp.zeros((D_MODEL, _W_IK_PAD), w_in.dtype)
    return jnp.concatenate([w_in[:, :split], pad, w_in[:, split:]], axis=1).astype(BF16)


def _pick_tile(n, pref):
    t = min(n, pref)
    while n % t:
        t //= 2
    return t


def _group(x, pos, dsa_fn, mem_k3, mem_v3, shift_prev, wkv_prev, t_valid, wts):
    b, t, _ = x.shape
    n = b * t
    x2 = x.reshape(n, D_MODEL)
    tm = _pick_tile(n, 256)
    qa, ka, va, iq, ikw, pb, cq, gs = _inproj(x2, pos, wts["norm_mix_g"], wts["w_in"], wts["b_gate"], tm)
    r3 = lambda a: a.reshape(b, t, a.shape[-1])
    oa = dsa_fn(r3(qa), r3(iq), r3(ikw), r3(ka), r3(va))
    chunk = _pick_tile(t, RWKV_CHUNK)
    nb = _pick_tile(b, max(2, 32 // chunk))
    ob, wkv_new = _rwkv(r3(pb), shift_prev, wkv_prev, wts["rwkv"], chunk, nb, t_valid)
    tq = _pick_tile(t, 512)
    oc = _xattn(r3(cq), mem_k3, mem_v3, tq, _pick_tile(b, max(1, 64 // tq)))
    y = _merge_mlp(x2, oa.reshape(n, W_A), ob.reshape(n, W_B), oc.reshape(n, W_C), gs,
                   wts["w_branch_a"], wts["w_branch_b"], wts["w_branch_c"], wts["w_out"],
                   wts["norm_mlp_g"], wts["w_mlp_up"], wts["w_mlp_down"], wts["norm_final_g"], tm)
    return y.reshape(b, t, D_MODEL), r3(ka), r3(va), r3(ikw)[..., :IDX_DIM], wkv_new, r3(pb)


def kernel(x_prompt, x_sample, mem_prompt, cache_k, cache_v, cache_idx_k, cache_mem_k, cache_mem_v,
           state_wkv, state_shift, page_table, norm_mix_g, norm_mem_g, norm_mlp_g, norm_final_g,
           w_in, b_gate, w_mem_kv, rwkv_mu, rwkv_w0, rwkv_w_up, rwkv_a0, rwkv_a_up, rwkv_g_up,
           rwkv_k_k, rwkv_k_a, rwkv_r_k, rwkv_ln_g, rwkv_ln_b, w_branch_a, w_branch_b, w_branch_c,
           w_out, w_mlp_up, w_mlp_down):
    bf = lambda w: w.astype(BF16)
    wts = dict(
        norm_mix_g=norm_mix_g, norm_mlp_g=norm_mlp_g, norm_final_g=norm_final_g,
        w_in=_pack_w_in(w_in), b_gate=b_gate,
        rwkv=dict(mu=rwkv_mu, w0=rwkv_w0, w_up=rwkv_w_up, a0=rwkv_a0, a_up=rwkv_a_up, g_up=rwkv_g_up,
                  k_k=rwkv_k_k, k_a=rwkv_k_a, r_k=rwkv_r_k, ln_g=rwkv_ln_g, ln_b=rwkv_ln_b),
        w_branch_a=bf(w_branch_a), w_branch_b=bf(w_branch_b), w_branch_c=bf(w_branch_c),
        w_out=bf(w_out), w_mlp_up=bf(w_mlp_up), w_mlp_down=bf(w_mlp_down))

    b, t, _ = x_prompt.shape
    n_mem = mem_prompt.shape[1]
    mem2 = mem_prompt.reshape(b * n_mem, D_MODEL)
    mk, mv = _memkv(mem2, norm_mem_g, bf(w_mem_kv), _pick_tile(b * n_mem, 256))
    mk3, mv3 = (a.reshape(b, n_mem * N_HEADS_C, HEAD_DIM_C) for a in (mk, mv))
    shift0 = jnp.zeros((b, RWKV_PROJ), F32)
    wkv0 = jnp.zeros((b, N_HEADS_B, HEAD_DIM_B, HEAD_DIM_B), F32)
    dsa_p = functools.partial(_dsa_prompt, tq=_pick_tile(t, 256))
    y_p, k_p, v_p, ik_p, wkv_p, pb_p = _group(
        x_prompt, jnp.arange(t), dsa_p, mk3, mv3, shift0, wkv0, None, wts)

    db, tn, _ = x_sample.shape
    n_pages = page_table.shape[1]
    page = cache_idx_k.shape[1]
    past = n_pages * page
    tp = -(-tn // SUBLANES) * SUBLANES
    x_s = jnp.pad(x_sample, ((0, 0), (0, tp - tn), (0, 0)))
    dsa_s = functools.partial(_dsa_sample, cache_idx_k=cache_idx_k, cache_k=cache_k, cache_v=cache_v,
                              page_table=page_table, t_new=tn)
    y_s, k_s, v_s, ik_s, wkv_s, pb_s = _group(
        x_s, past + jnp.arange(tp), dsa_s, cache_mem_k.reshape(db, n_mem * N_HEADS_C, HEAD_DIM_C),
        cache_mem_v.reshape(db, n_mem * N_HEADS_C, HEAD_DIM_C), state_shift, state_wkv, tn, wts)

    heads = lambda a: a.reshape(a.shape[0], a.shape[1], N_KV_A, HEAD_DIM_A)
    memh = lambda a: a.reshape(b, n_mem, N_HEADS_C, HEAD_DIM_C)
    return (y_p, y_s[:, :tn], heads(k_p), heads(v_p), ik_p, memh(mk3), memh(mv3), wkv_p, pb_p[:, -1],
            heads(k_s[:, :tn]), heads(v_s[:, :tn]), ik_s[:, :tn], wkv_s, pb_s[:, tn - 1])
```

```python
import functools
import math

import jax
import jax.numpy as jnp
from jax import lax
from jax.experimental import pallas as pl
from jax.experimental.pallas import tpu as pltpu

F32 = jnp.float32
BF16 = jnp.bfloat16

D_MODEL = 1024
N_HEADS_A, N_KV_A, HEAD_DIM_A = 8, 2, 64
W_A = N_HEADS_A * HEAD_DIM_A
W_KV = N_KV_A * HEAD_DIM_A
N_IDX_HEADS, IDX_DIM = 4, 64
W_IQ = N_IDX_HEADS * IDX_DIM
TOPK_MAX = 256
ROPE_THETA = 500000.0
ROT_DIM = 16
N_HEADS_B, HEAD_DIM_B = 8, 64
W_B = N_HEADS_B * HEAD_DIM_B
D_DECAY_LORA, D_AAA_LORA, D_GATE_LORA = 64, 64, 128
RWKV_PROJ = 3 * W_B + D_DECAY_LORA + D_AAA_LORA + D_GATE_LORA
GN_EPS = 64e-5
N_HEADS_C, HEAD_DIM_C = 4, 128
W_C = N_HEADS_C * HEAD_DIM_C
N_BRANCH = 3
D_FF = 4 * D_MODEL
RMS_EPS = 1e-6

LANES = 128
SUBLANES = 8
VMEM_LIMIT_BYTES = 56 * 1024 * 1024

INT_MIN = -(2 ** 31)
I16_MIN = -(2 ** 15)

RWKV_CHUNK = 128

_W_IK_PAD = LANES - IDX_DIM - N_IDX_HEADS
SEG_QA = (0, W_A)
SEG_KA = (W_A, W_KV)
SEG_VA = (W_A + W_KV, W_KV)
SEG_IQ = (W_A + 2 * W_KV, W_IQ)
SEG_IKW = (SEG_IQ[0] + W_IQ, LANES)
SEG_PB = (SEG_IKW[0] + LANES, RWKV_PROJ)
SEG_CQ = (SEG_PB[0] + RWKV_PROJ, W_C)
SEG_G = (SEG_CQ[0] + W_C, N_BRANCH * D_MODEL)
W_PACKED = SEG_G[0] + SEG_G[1]


def _rmsnorm(x, g):
    ms = jnp.mean(x * x, axis=-1, keepdims=True)
    return x * lax.rsqrt(ms + RMS_EPS) * g


def _rope(y, cos, s_lo, s_hi):
    return y * cos + pltpu.roll(y, LANES - ROT_DIM // 2, axis=1) * s_lo + pltpu.roll(y, ROT_DIM // 2, axis=1) * s_hi


def _inproj_kernel(x_ref, g_ref, w_ref, bg_ref, cos_ref, slo_ref, shi_ref,
                   qa_ref, ka_ref, va_ref, iq_ref, ikw_ref, pb_ref, cq_ref, gs_ref):
    xn = _rmsnorm(x_ref[...], g_ref[...]).astype(BF16)
    cos, s_lo, s_hi = cos_ref[...], slo_ref[...], shi_ref[...]

    def proj(off, width):
        return jnp.dot(xn, w_ref[:, off:off + width], preferred_element_type=F32)

    def rope_into(out_ref, c, y):
        for j in range(0, y.shape[1], LANES):
            out_ref[:, c + j:c + j + LANES] = _rope(y[:, j:j + LANES], cos, s_lo, s_hi)

    for c in range(0, SEG_QA[1], 2 * LANES):
        rope_into(qa_ref, c, proj(SEG_QA[0] + c, 2 * LANES))
    kv = proj(SEG_KA[0], SEG_KA[1] + SEG_VA[1])
    rope_into(ka_ref, 0, kv[:, :SEG_KA[1]])
    va_ref[...] = kv[:, SEG_KA[1]:]
    rope_into(iq_ref, 0, proj(*SEG_IQ))
    y = proj(*SEG_IKW)
    lane = lax.broadcasted_iota(jnp.int32, y.shape, 1)
    ikw_ref[...] = jnp.where(lane < IDX_DIM, _rope(y, cos, s_lo, s_hi), y * (W_IQ ** -0.5))
    for c in range(0, SEG_PB[1], 256):
        pb_ref[:, c:c + 256] = proj(SEG_PB[0] + c, 256)
    cq_ref[...] = proj(*SEG_CQ)
    for c in range(0, SEG_G[1], 512):
        gs_ref[:, c:c + 512] = jax.nn.sigmoid(proj(SEG_G[0] + c, 512) + bg_ref[:, c:c + 512])


def _rope_tables(pos, rows):
    half = ROT_DIM // 2
    inv_freq = ROPE_THETA ** (-jnp.arange(half, dtype=F32) / half)
    ang = pos.astype(F32)[:, None] * inv_freq[None, :]
    cos, sin = jnp.cos(ang), jnp.sin(ang)
    t = pos.shape[0]
    ones = jnp.ones((t, HEAD_DIM_A - ROT_DIM), F32)
    zeros_h = jnp.zeros((t, half), F32)
    zeros_r = jnp.zeros((t, HEAD_DIM_A - ROT_DIM), F32)
    c64 = jnp.concatenate([cos, cos, ones], axis=1)
    lo64 = jnp.concatenate([-sin, zeros_h, zeros_r], axis=1)
    hi64 = jnp.concatenate([zeros_h, sin, zeros_r], axis=1)
    reps = rows // t
    return tuple(jnp.tile(jnp.concatenate([a, a], axis=1), (reps, 1)) for a in (c64, lo64, hi64))


def _inproj(x2, pos, norm_g, w_packed, b_gate, tm):
    n = x2.shape[0]
    t = pos.shape[0]
    rows = max(t, tm)
    cos, s_lo, s_hi = _rope_tables(pos, rows)
    nt = rows // tm
    row = lambda i: (i, 0)
    const = lambda i: (0, 0)
    tab = pl.BlockSpec((tm, LANES), lambda i: (i % nt, 0))
    widths = [SEG_QA[1], SEG_KA[1], SEG_VA[1], SEG_IQ[1], SEG_IKW[1], SEG_PB[1], SEG_CQ[1], SEG_G[1]]
    return pl.pallas_call(
        _inproj_kernel,
        grid=(n // tm,),
        in_specs=[
            pl.BlockSpec((tm, D_MODEL), row),
            pl.BlockSpec((1, D_MODEL), const),
            pl.BlockSpec((D_MODEL, W_PACKED), const, pipeline_mode=pl.Buffered(1)),
            pl.BlockSpec((1, SEG_G[1]), const),
            tab, tab, tab,
        ],
        out_specs=[pl.BlockSpec((tm, w), row) for w in widths],
        out_shape=[jax.ShapeDtypeStruct((n, w), F32) for w in widths],
        compiler_params=pltpu.CompilerParams(
            dimension_semantics=("arbitrary",), vmem_limit_bytes=VMEM_LIMIT_BYTES),
        name="inproj",
    )(x2, norm_g.reshape(1, D_MODEL), w_packed, b_gate.reshape(1, -1), cos, s_lo, s_hi)


def _memkv_kernel(x_ref, g_ref, w_ref, k_ref, v_ref):
    xn = _rmsnorm(x_ref[...], g_ref[...]).astype(BF16)
    tm = x_ref.shape[0]
    for out_ref, off in ((k_ref, 0), (v_ref, W_C)):
        y = jnp.dot(xn, w_ref[:, off:off + W_C], preferred_element_type=F32)
        for h in range(N_HEADS_C):
            out_ref[pl.ds(h, tm, stride=N_HEADS_C), :] = y[:, h * HEAD_DIM_C:(h + 1) * HEAD_DIM_C]


def _memkv(mem2, norm_g, w_bf16, tm):
    n = mem2.shape[0]
    row = lambda i: (i, 0)
    const = lambda i: (0, 0)
    return pl.pallas_call(
        _memkv_kernel,
        grid=(n // tm,),
        in_specs=[pl.BlockSpec((tm, D_MODEL), row), pl.BlockSpec((1, D_MODEL), const),
                  pl.BlockSpec((D_MODEL, 2 * W_C), const)],
        out_specs=[pl.BlockSpec((tm * N_HEADS_C, HEAD_DIM_C), row)] * 2,
        out_shape=[jax.ShapeDtypeStruct((n * N_HEADS_C, HEAD_DIM_C), F32)] * 2,
        compiler_params=pltpu.CompilerParams(dimension_semantics=("arbitrary",)),
        name="memkv",
    )(mem2, norm_g.reshape(1, D_MODEL), w_bf16)


def _xattn_kernel(q_ref, k_ref, v_ref, o_ref):
    n_mem = k_ref.shape[1] // N_HEADS_C
    units = [(g, h) for g in range(q_ref.shape[0]) for h in range(N_HEADS_C)]
    head = lambda ref, g, h: ref[g, pl.ds(h, n_mem, stride=N_HEADS_C), :].astype(BF16)
    cols = lambda h: slice(h * HEAD_DIM_C, (h + 1) * HEAD_DIM_C)
    s = [lax.dot_general(q_ref[g, :, cols(h)].astype(BF16), head(k_ref, g, h), (((1,), (1,)), ((), ())),
                         preferred_element_type=F32) * (HEAD_DIM_C ** -0.5) for g, h in units]
    p = []
    for sj in s:
        e = jnp.exp(sj - jnp.max(sj, axis=-1, keepdims=True))
        p.append((e / jnp.sum(e, axis=-1, keepdims=True)).astype(BF16))
    for pj, (g, h) in zip(p, units):
        o_ref[g, :, cols(h)] = jnp.dot(pj, head(v_ref, g, h), preferred_element_type=F32)


def _xattn(cq3, mk3, mv3, tq, nb):
    b, t, _ = cq3.shape
    rows = mk3.shape[1]
    return pl.pallas_call(
        _xattn_kernel,
        grid=(b // nb, t // tq),
        in_specs=[pl.BlockSpec((nb, tq, W_C), lambda i, j: (i, j, 0)),
                  pl.BlockSpec((nb, rows, HEAD_DIM_C), lambda i, j: (i, 0, 0)),
                  pl.BlockSpec((nb, rows, HEAD_DIM_C), lambda i, j: (i, 0, 0))],
        out_specs=pl.BlockSpec((nb, tq, W_C), lambda i, j: (i, j, 0)),
        out_shape=jax.ShapeDtypeStruct((b, t, W_C), F32),
        compiler_params=pltpu.CompilerParams(dimension_semantics=("arbitrary", "arbitrary")),
        name="xattn",
    )(cq3, mk3, mv3)


def _merge_mlp_kernel(x_ref, oa_ref, ob_ref, oc_ref, gs_ref, wa_ref, wb_ref, wc_ref, wo_ref,
                      gm_ref, wu_ref, wd_ref, gf_ref, y_ref):
    def bdot(a, w):
        return jnp.dot(a.astype(BF16), w, preferred_element_type=F32)

    merged = (gs_ref[:, :D_MODEL] * bdot(oa_ref[...], wa_ref[...])
              + gs_ref[:, D_MODEL:2 * D_MODEL] * bdot(ob_ref[...], wb_ref[...])
              + gs_ref[:, 2 * D_MODEL:] * bdot(oc_ref[...], wc_ref[...]))
    h = x_ref[...] + bdot(merged, wo_ref[...])
    hn = _rmsnorm(h, gm_ref[...]).astype(BF16)
    acc = h
    for c in range(0, D_FF, 1024):
        u = jnp.dot(hn, wu_ref[:, c:c + 1024], preferred_element_type=F32)
        acc = acc + bdot(jnp.square(jnp.maximum(u, 0.0)), wd_ref[c:c + 1024, :])
    y_ref[...] = _rmsnorm(acc, gf_ref[...])


def _merge_mlp(x2, oa, ob, oc, gs, wa, wb, wc, wo, g_mlp, wu, wd, g_final, tm):
    n = x2.shape[0]
    row = lambda i: (i, 0)
    const = lambda i: (0, 0)
    resident = lambda shape: pl.BlockSpec(shape, const, pipeline_mode=pl.Buffered(1))
    return pl.pallas_call(
        _merge_mlp_kernel,
        grid=(n // tm,),
        in_specs=[
            pl.BlockSpec((tm, D_MODEL), row),
            pl.BlockSpec((tm, W_A), row), pl.BlockSpec((tm, W_B), row), pl.BlockSpec((tm, W_C), row),
            pl.BlockSpec((tm, N_BRANCH * D_MODEL), row),
            resident((W_A, D_MODEL)), resident((W_B, D_MODEL)), resident((W_C, D_MODEL)),
            resident((D_MODEL, D_MODEL)),
            pl.BlockSpec((1, D_MODEL), const),
            resident((D_MODEL, D_FF)), resident((D_FF, D_MODEL)),
            pl.BlockSpec((1, D_MODEL), const),
        ],
        out_specs=pl.BlockSpec((tm, D_MODEL), row),
        out_shape=jax.ShapeDtypeStruct((n, D_MODEL), F32),
        compiler_params=pltpu.CompilerParams(
            dimension_semantics=("arbitrary",), vmem_limit_bytes=VMEM_LIMIT_BYTES),
        name="merge_mlp",
    )(x2, oa, ob, oc, gs, wa, wb, wc, wo, g_mlp.reshape(1, -1), wu, wd, g_final.reshape(1, -1))


def _bf16_terms(x, n):
    terms = []
    for _ in range(n):
        t = x.astype(BF16)
        terms.append(t)
        x = x - t.astype(F32)
    return terms


def _dot_split(a, b_terms):
    (ah, al), (bh, bl) = _bf16_terms(a, 2), b_terms
    d = lambda x, y: jnp.dot(x, y, preferred_element_type=F32)
    return d(ah, bh) + (d(ah, bl) + d(al, bh))


def _head_sums(x):
    lo = lax.broadcasted_iota(jnp.int32, (x.shape[0], LANES), 1) < HEAD_DIM_B
    out = []
    for c in range(0, x.shape[1], LANES):
        xc = x[:, c:c + LANES]
        s_lo = jnp.sum(jnp.where(lo, xc, 0.0), axis=-1, keepdims=True)
        s_hi = jnp.sum(jnp.where(lo, 0.0, xc), axis=-1, keepdims=True)
        out.append(jnp.where(lo, s_lo, s_hi))
    return jnp.concatenate(out, axis=1)


def _rwkv_kernel(pb_ref, sh_ref, s0_ref, mu_ref, w0_ref, wup_ref, a0_ref, aup_ref, gup_ref,
                 kk_ref, ka_ref, rk_ref, lng_ref, lnb_ref, ob_ref, sout_ref, carry_ref, state_ref,
                 *, chunk, t_valid):
    c = pl.program_id(1)
    nb = pb_ref.shape[0]

    @pl.when(c == 0)
    def _():
        carry_ref[...] = sh_ref[...]
        state_ref[...] = s0_ref[...]

    row1 = lax.broadcasted_iota(jnp.int32, (chunk, 1), 0)
    rr = lax.broadcasted_iota(jnp.int32, (chunk, chunk), 0)
    cc = lax.broadcasted_iota(jnp.int32, (chunk, chunk), 1)
    strict, incl = rr > cc, rr >= cc
    tri = incl.astype(BF16)
    w_up, a_up, g_up = (_bf16_terms(w[...], 2) for w in (wup_ref, aup_ref, gup_ref))

    def prepare(bi):
        pbc = pb_ref[bi]
        prev = jnp.where(row1 == 0, carry_ref[bi], pltpu.roll(pbc, 1, axis=0))
        carry_ref[bi] = pbc[chunk - 1:chunk, :]
        ps = pbc + (prev - pbc) * mu_ref[...]
        r, k, v = ps[:, :W_B], ps[:, W_B:2 * W_B], ps[:, 2 * W_B:3 * W_B]
        o = 3 * W_B
        wl = ps[:, o:o + D_DECAY_LORA]
        al = ps[:, o + D_DECAY_LORA:o + D_DECAY_LORA + D_AAA_LORA]
        gl = ps[:, o + D_DECAY_LORA + D_AAA_LORA:]
        z = -(w0_ref[...] + _dot_split(jnp.tanh(wl), w_up))
        softplus = jnp.maximum(z, 0.0) + jnp.log(1.0 + jnp.exp(-jnp.abs(z)))
        ld = -jnp.exp(-softplus - 0.5)
        alpha = jax.nn.sigmoid(a0_ref[...] + _dot_split(al, a_up))
        gate = _dot_split(jax.nn.sigmoid(gl), g_up)
        kkf = k * kk_ref[...]
        khf = k * (1.0 + (alpha - 1.0) * ka_ref[...])
        if t_valid is not None:
            valid = (row1 + c * chunk) < t_valid
            ld = jnp.where(valid, ld, 0.0)
            alpha = jnp.where(valid, alpha, 0.0)
            khf = jnp.where(valid, khf, 0.0)
        cum = sum(jnp.dot(tri, part, preferred_element_type=F32) for part in _bf16_terms(ld, 3))
        ecum, einv, eprev = jnp.exp(cum), jnp.exp(-cum), jnp.exp(cum - ld)
        kkn = kkf * lax.rsqrt(jnp.maximum(_head_sums(kkf * kkf), 1e-24))
        b_t = kkn * alpha * einv
        k_t = khf * einv
        g_c = ecum[chunk - 1:chunk, :]
        return dict(r=r, v=v, khf=khf, gate=gate, g_c=g_c,
                    a_t=(-kkn * eprev).astype(BF16), r_t=(r * ecum).astype(BF16),
                    b_c=(b_t * g_c).astype(BF16), k_c=(k_t * g_c).astype(BF16),
                    b_t=b_t.astype(BF16), k_t=k_t.astype(BF16), vb=v.astype(BF16))

    def mm(a, b):
        return jnp.dot(a.astype(BF16), b.astype(BF16), preferred_element_type=F32)

    def mm_t(a, b):
        return lax.dot_general(a.astype(BF16), b.astype(BF16), (((1,), (1,)), ((), ())),
                               preferred_element_type=F32)

    rows = [prepare(bi) for bi in range(nb)]
    n_it = max(1, int(math.ceil(math.log2(chunk))))
    units = [(bi, h, slice(h * HEAD_DIM_B, (h + 1) * HEAD_DIM_B))
             for bi in range(nb) for h in range(N_HEADS_B)]
    idx = range(len(units))
    col = lambda name: [rows[bi][name][:, sl] for bi, _, sl in units]
    a_t, r_t, b_t, k_t, b_c, k_c, vb = (col(n) for n in ("a_t", "r_t", "b_t", "k_t", "b_c", "k_c", "vb"))
    x1 = [jnp.concatenate([a_t[j], r_t[j]], axis=0) for j in idx]
    s0 = [state_ref[bi, h] for bi, h, _ in units]
    p0 = [mm_t(x1[j], s0[j]) for j in idx]
    if chunk % LANES == 0:
        g = [mm_t(x1[j], jnp.concatenate([b_t[j], k_t[j]], axis=0)) for j in idx]
        incl2 = jnp.concatenate([incl, incl], axis=1)
        u = [p0[j][:chunk] + mm(jnp.where(strict, g[j][:chunk, chunk:], 0.0), vb[j]) for j in idx]
        lp = [jnp.where(strict, g[j][:chunk, :chunk], 0.0).astype(BF16) for j in idx]
        for it in range(n_it - 1):
            sq = [mm(lp[j], jnp.concatenate([lp[j], u[j].astype(BF16)], axis=1)) for j in idx]
            u = [u[j] + sq[j][:, chunk:] for j in idx]
            lp = [sq[j][:, :chunk].astype(BF16) for j in idx]
        u = [u[j] + mm(lp[j], u[j]) for j in idx]
        uv = [jnp.concatenate([u[j].astype(BF16), vb[j]], axis=0) for j in idx]
        y = [p0[j][chunk:] + mm(jnp.where(incl2, g[j][chunk:], 0.0), uv[j]) for j in idx]
    else:
        g_b = [mm_t(x1[j], b_t[j]) for j in idx]
        g_k = [mm_t(x1[j], k_t[j]) for j in idx]
        u = [p0[j][:chunk] + mm(jnp.where(strict, g_k[j][:chunk], 0.0), vb[j]) for j in idx]
        lp = [jnp.where(strict, g_b[j][:chunk], 0.0).astype(BF16) for j in idx]
        for it in range(n_it):
            u = [u[j] + mm(lp[j], u[j]) for j in idx]
            if it + 1 < n_it:
                lp = [mm(lp[j], lp[j]).astype(BF16) for j in idx]
        y = [p0[j][chunk:] + mm(jnp.where(incl, g_b[j][chunk:], 0.0), u[j])
             + mm(jnp.where(incl, g_k[j][chunk:], 0.0), vb[j]) for j in idx]
        uv = [jnp.concatenate([u[j].astype(BF16), vb[j]], axis=0) for j in idx]
    for j, (bi, h, sl) in enumerate(units):
        bk = jnp.concatenate([b_c[j], k_c[j]], axis=0)
        state_ref[bi, h] = s0[j] * rows[bi]["g_c"][:, sl] + lax.dot_general(
            uv[j], bk, (((0,), (0,)), ((), ())), preferred_element_type=F32)

    for bi, row in enumerate(rows):
        y_all = jnp.concatenate(y[bi * N_HEADS_B:(bi + 1) * N_HEADS_B], axis=1)
        d = y_all - _head_sums(y_all) * (1.0 / HEAD_DIM_B)
        var = _head_sums(d * d) * (1.0 / HEAD_DIM_B)
        yn = d * lax.rsqrt(var + GN_EPS) * lng_ref[...] + lnb_ref[...]
        bonus = _head_sums(row["r"] * row["khf"] * rk_ref[...]) * row["v"]
        ob_ref[bi] = (yn + bonus) * row["gate"]

    @pl.when(c == pl.num_programs(1) - 1)
    def _():
        sout_ref[...] = state_ref[...]


def _rwkv(pb3, shift_prev, wkv_prev, p, chunk, nb, t_valid):
    b, t, _ = pb3.shape
    const = lambda i, j: (0, 0)
    vec = lambda n: pl.BlockSpec((1, n), const)
    state_spec = pl.BlockSpec((nb, N_HEADS_B, HEAD_DIM_B, HEAD_DIM_B), lambda i, j: (i, 0, 0, 0))
    return pl.pallas_call(
        functools.partial(_rwkv_kernel, chunk=chunk, t_valid=t_valid),
        grid=(b // nb, t // chunk),
        in_specs=[
            pl.BlockSpec((nb, chunk, RWKV_PROJ), lambda i, j: (i, j, 0)),
            pl.BlockSpec((nb, 1, RWKV_PROJ), lambda i, j: (i, 0, 0)),
            state_spec,
            vec(RWKV_PROJ), vec(W_B), pl.BlockSpec((D_DECAY_LORA, W_B), const),
            vec(W_B), pl.BlockSpec((D_AAA_LORA, W_B), const), pl.BlockSpec((D_GATE_LORA, W_B), const),
            vec(W_B), vec(W_B), vec(W_B), vec(W_B), vec(W_B),
        ],
        out_specs=[pl.BlockSpec((nb, chunk, W_B), lambda i, j: (i, j, 0)), state_spec],
        out_shape=[jax.ShapeDtypeStruct((b, t, W_B), F32),
                   jax.ShapeDtypeStruct(wkv_prev.shape, F32)],
        scratch_shapes=[pltpu.VMEM((nb, 1, RWKV_PROJ), F32),
                        pltpu.VMEM((nb, N_HEADS_B, HEAD_DIM_B, HEAD_DIM_B), F32)],
        compiler_params=pltpu.CompilerParams(dimension_semantics=("arbitrary", "arbitrary")),
        name="rwkv",
    )(pb3, shift_prev.reshape(b, 1, RWKV_PROJ), wkv_prev,
      p["mu"].reshape(1, -1), p["w0"].reshape(1, -1), p["w_up"], p["a0"].reshape(1, -1), p["a_up"],
      p["g_up"], p["k_k"].reshape(1, -1), p["k_a"].reshape(1, -1), p["r_k"].reshape(1, -1),
      p["ln_g"].reshape(1, -1), p["ln_b"].reshape(1, -1))


def _f32_to_key(s):
    b = lax.bitcast_convert_type(s, jnp.int32)
    return b ^ (lax.shift_right_arithmetic(b, 31) & 0x7FFFFFFF)


def _key_to_f32(k):
    return lax.bitcast_convert_type(k ^ (lax.shift_right_arithmetic(k, 31) & 0x7FFFFFFF), F32)


def _confirm_kth(count_ge, guess, live, topk):
    def n_true(m):
        return jnp.sum((m & live).astype(jnp.int32))

    def widen(c):
        lo, hi, step, _ = c
        low = count_ge(_key_to_f32(lo)) < topk
        high = count_ge(_key_to_f32(hi)) >= topk
        return (jnp.where(low, lo - step, lo), jnp.where(high, hi + step, hi), step * 2,
                n_true(low | high))

    def halve(c):
        lo, hi, _ = c
        mid = lo + lax.shift_right_arithmetic(hi - lo, 1)
        ok = count_ge(_key_to_f32(mid)) >= topk
        is_open = hi - lo > 1
        lo, hi = jnp.where(is_open & ok, mid, lo), jnp.where(is_open & ~ok, mid, hi)
        return lo, hi, n_true(hi - lo > 1)

    lo, hi, _, _ = lax.while_loop(lambda c: c[3] > 0, widen, (guess, guess + 1, jnp.int32(1 << 12), jnp.int32(1)))
    lo, _, _ = lax.while_loop(lambda c: c[2] > 0, halve, (lo, hi, n_true(hi - lo > 1)))
    return lo


def _tie_cutoff(count_eq_before, shape, need, n_bits):
    def body(it, x):
        cand = x + lax.shift_left(jnp.int32(1), n_bits - 1 - it)
        return jnp.where(count_eq_before(cand) < need, cand, x)
    return lax.fori_loop(0, n_bits, body, jnp.zeros(shape, jnp.int32))


def _dsa_prompt_kernel(qa_ref, iq_ref, ikwq_ref, ka_ref, va_ref, ikw_ref, o_ref,
                       sc_ref, kh_ref, kl_ref, cnt_ref, cnt16_ref, tied_ref, m_ref, l_ref, acc_ref, *, tq, tk, topk):
    i = pl.program_id(1)
    n_full = ((i + 1) * tq) // tk
    has_part = ((i + 1) * tq) % tk != 0
    iq = iq_ref[0]
    iq_heads = jnp.concatenate(
        [iq[:, h * IDX_DIM:(h + 1) * IDX_DIM] for h in range(N_IDX_HEADS)], axis=0).astype(BF16)
    w_t = ikwq_ref[0].T
    sizes = sorted({tk, tq})
    kpos0 = {n: lax.broadcasted_iota(jnp.int32, (n, tq), 0) for n in sizes}
    qpos = {n: i * tq + lax.broadcasted_iota(jnp.int32, (n, tq), 1) for n in sizes}
    dn_t = (((1,), (1,)), ((), ()))
    dn_0 = (((0,), (0,)), ((), ()))

    def for_tiles(body, init, carry_ref=None):
        def full(kt, c):
            return body(pl.ds(pl.multiple_of(kt * tk, tk), tk), tk, c)
        c = lax.fori_loop(0, n_full, full, init)
        if tk == tq:
            return c
        last = pl.ds(pl.multiple_of(n_full * tk, tq), tq)
        if carry_ref is None:
            @pl.when(has_part)
            def _():
                body(last, tq, c)
            return c
        carry_ref[...] = c

        @pl.when(has_part)
        def _():
            carry_ref[...] = body(last, tq, carry_ref[...])
        return carry_ref[...]

    def scores(rows, n, carry):
        ik = ikw_ref[0, rows, :][:, :IDX_DIM].astype(BF16)
        d = lax.dot_general(ik, iq_heads, dn_t, preferred_element_type=F32)
        s = jnp.zeros((n, tq), F32)
        for h in range(N_IDX_HEADS):
            s = s + w_t[IDX_DIM + h:IDX_DIM + h + 1, :] * jnp.maximum(d[:, h * tq:(h + 1) * tq], 0.0)
        sc = jnp.where(kpos0[n] + rows.start <= qpos[n], s + 0.0, -jnp.inf)
        sc_ref[rows, :] = sc
        key = _f32_to_key(sc)
        kh_ref[rows, :] = lax.shift_right_arithmetic(key, 16).astype(jnp.int16)
        kl_ref[rows, :] = ((key & 0xFFFF) + I16_MIN).astype(jnp.int16)
        return carry

    for_tiles(scores, 0)

    def count(pred):
        def body(rows, n, acc):
            m = pred(sc_ref[rows, :]).astype(jnp.int32)
            return acc + jnp.sum(m.reshape(n // SUBLANES, SUBLANES, tq), axis=0)
        acc = for_tiles(body, jnp.zeros((SUBLANES, tq), jnp.int32), cnt_ref)
        return jnp.sum(acc, axis=0, keepdims=True)

    rows16 = 2 * SUBLANES

    def count16(ref, pred):
        def body(rows, n, acc):
            one = jnp.where(pred(ref[rows, :]), jnp.int16(1), jnp.int16(0))
            part = one[0:rows16, :]
            for j in range(rows16, n, rows16):
                part = part + one[j:j + rows16, :]
            return acc + part.astype(jnp.int32)
        acc = for_tiles(body, jnp.zeros((rows16, tq), jnp.int32), cnt16_ref)
        return jnp.sum(acc, axis=0, keepdims=True)

    def kth16(ref, above):
        def body(it, t):
            cand = t + lax.shift_left(jnp.int32(1), 15 - it)
            c16 = cand.astype(jnp.int16)
            return jnp.where(above + count16(ref, lambda blk: blk >= c16) >= topk, cand, t)
        return lax.fori_loop(0, 16, body, jnp.full((1, tq), I16_MIN, jnp.int32))

    t_hi = kth16(kh_ref, 0)
    h16 = t_hi.astype(jnp.int16)
    n_hi = count16(kh_ref, lambda blk: blk > h16)

    def keep_equal(rows, n, carry):
        kl_ref[rows, :] = jnp.where(kh_ref[rows, :] == h16, kl_ref[rows, :], jnp.int16(I16_MIN))
        return carry

    for_tiles(keep_equal, 0)
    guess = lax.shift_left(t_hi, 16) + (kth16(kl_ref, n_hi) - I16_MIN)
    live = i * tq + lax.broadcasted_iota(jnp.int32, (1, tq), 1) + 1 >= topk
    k_thr = _confirm_kth(lambda c: count(lambda blk: blk >= c), guess, live, topk)
    thr = jnp.where(live, _key_to_f32(k_thr), -jnp.inf)
    need = (topk - count(lambda blk: blk > thr)).astype(F32)
    before = {n: (lax.broadcasted_iota(jnp.int32, (n, n), 0)
                  > lax.broadcasted_iota(jnp.int32, (n, n), 1)).astype(BF16) for n in sizes}
    ones = {n: jnp.ones((n, SUBLANES), BF16) for n in sizes}

    def select(rows, n, tied_before):
        blk = sc_ref[rows, :]
        tied = (blk == thr) & live
        tied_b = jnp.where(tied, 1.0, 0.0).astype(BF16)
        rank = tied_before + jnp.dot(before[n], tied_b, preferred_element_type=F32)
        sel = (blk > thr) | (tied & (rank < need))
        sc_ref[rows, :] = jnp.where(sel, 0.0, -jnp.inf)
        return tied_before + lax.dot_general(ones[n], tied_b, dn_0, preferred_element_type=F32)[:1, :]

    for_tiles(select, jnp.zeros((1, tq), F32), tied_ref)

    m_ref[...] = jnp.full(m_ref.shape, -jnp.inf, F32)
    l_ref[...] = jnp.zeros(l_ref.shape, F32)
    acc_ref[...] = jnp.zeros(acc_ref.shape, F32)
    qa = (qa_ref[0] * (HEAD_DIM_A ** -0.5 * math.log2(math.e))).astype(BF16)
    heads = range(N_HEADS_A)
    low_lanes = {n: lax.broadcasted_iota(jnp.int32, (n, W_KV), 1) < HEAD_DIM_A for n in sizes}
    hs = [slice(h * HEAD_DIM_A, (h + 1) * HEAD_DIM_A) for h in heads]
    gs = [slice((h // (N_HEADS_A // N_KV_A)) * HEAD_DIM_A, (h // (N_HEADS_A // N_KV_A) + 1) * HEAD_DIM_A)
          for h in heads]

    def attend(rows, n, carry):
        bias = sc_ref[rows, :]
        kk = ka_ref[0, rows, :].astype(BF16)
        v32 = va_ref[0, rows, :]
        v_aug = [jnp.where(low_lanes[n], v32 if g == 0 else pltpu.roll(v32, HEAD_DIM_A, axis=1), 1.0)
                 .astype(BF16) for g in range(N_KV_A)]
        s = [lax.dot_general(kk[:, gs[h]], qa[:, hs[h]], dn_t, preferred_element_type=F32) + bias
             for h in heads]
        for h in heads:
            m_old = m_ref[h:h + 1, :]
            m_new = jnp.maximum(m_old, jnp.max(s[h], axis=0, keepdims=True))
            m_safe = jnp.where(m_new == -jnp.inf, 0.0, m_new)
            alpha = jnp.exp2(m_old - m_safe)
            p = jnp.exp2((s[h] - m_safe).astype(BF16))
            m_ref[h:h + 1, :] = m_new
            pv = lax.dot_general(v_aug[h // (N_HEADS_A // N_KV_A)], p, dn_0, preferred_element_type=F32)
            l_ref[h:h + 1, :] = alpha * l_ref[h:h + 1, :] + pv[HEAD_DIM_A:HEAD_DIM_A + 1, :]
            acc_ref[hs[h], :] = alpha * acc_ref[hs[h], :] + pv[:HEAD_DIM_A, :]
        return carry

    for_tiles(attend, 0)
    for h in heads:
        acc_ref[hs[h], :] = acc_ref[hs[h], :] / l_ref[h:h + 1, :]
    o_ref[0] = acc_ref[...].T


def _dsa_prompt(qa3, iq3, ikw3, ka3, va3, tq):
    b, t, _ = qa3.shape
    topk = min(TOPK_MAX, t // 4)
    assert tq >= topk and t % tq == 0
    tk = 2 * tq if t % (2 * tq) == 0 else tq
    qtile = lambda w: pl.BlockSpec((1, tq, w), lambda i, j: (i, j, 0))
    whole = lambda w: pl.BlockSpec((1, t, w), lambda i, j: (i, 0, 0))
    return pl.pallas_call(
        functools.partial(_dsa_prompt_kernel, tq=tq, tk=tk, topk=topk),
        grid=(b, t // tq),
        in_specs=[qtile(W_A), qtile(W_IQ), qtile(LANES), whole(W_KV), whole(W_KV), whole(LANES)],
        out_specs=qtile(W_A),
        out_shape=jax.ShapeDtypeStruct((b, t, W_A), F32),
        scratch_shapes=[pltpu.VMEM((t, tq), F32),
                        pltpu.VMEM((t, tq), jnp.int16), pltpu.VMEM((t, tq), jnp.int16),
                        pltpu.VMEM((SUBLANES, tq), jnp.int32), pltpu.VMEM((2 * SUBLANES, tq), jnp.int32),
                        pltpu.VMEM((1, tq), F32),
                        pltpu.VMEM((N_HEADS_A, tq), F32), pltpu.VMEM((N_HEADS_A, tq), F32),
                        pltpu.VMEM((W_A, tq), F32)],
        compiler_params=pltpu.CompilerParams(
            dimension_semantics=("arbitrary", "arbitrary"), vmem_limit_bytes=VMEM_LIMIT_BYTES),
        name="dsa_prompt",
    )(qa3, iq3, ikw3, ka3, va3, ikw3)


def _dsa_sample_kernel(pt_ref, qa_ref, iq_ref, ikwn_ref, kan_ref, van_ref, cik_ref, ck_ref, cv_ref,
                       o_ref, ik_buf, k_buf, v_buf, sems, *, n_pages, page, tn, t_new, topk, n_bits):
    step = pl.program_id(0)
    n_steps = pl.num_programs(0)
    nr = qa_ref.shape[0]
    past = n_pages * page
    total = past + LANES
    slot = step % 2
    sem_k, sem_v = 2, 3

    def ik_copy(row, dst, r, j):
        return pltpu.make_async_copy(cik_ref.at[pt_ref[row, j]],
                                     ik_buf.at[dst, r, :, pl.ds(j * page, page)], sems.at[dst])

    def kv_copies(row, r, j):
        pg, cols = pt_ref[row, j], pl.ds(j * page, page)
        return (pltpu.make_async_copy(ck_ref.at[pg], k_buf.at[r, :, cols], sems.at[sem_k]),
                pltpu.make_async_copy(cv_ref.at[pg], v_buf.at[r, :, cols], sems.at[sem_v]))

    def start_ik(at_step, dst):
        for r in range(nr):
            for j in range(n_pages):
                ik_copy(at_step * nr + r, dst, r, j).start()

    @pl.when(step == 0)
    def _():
        start_ik(0, 0)

    for r in range(nr):
        for j in range(n_pages):
            for cp in kv_copies(step * nr + r, r, j):
                cp.start()

    @pl.when(step + 1 < n_steps)
    def _():
        start_ik(step + 1, 1 - slot)

    for r in range(nr):
        for j in range(n_pages):
            ik_copy(step * nr + r, slot, r, j).wait()

    def tail_tile(x):
        return jnp.concatenate([x, jnp.zeros((LANES - tn, x.shape[1]), F32)], axis=0).T

    rows_all = nr * tn
    keys = []
    for r in range(nr):
        ikw = ikwn_ref[r]
        ik_buf[slot, r, :, past:] = tail_tile(ikw)[:IDX_DIM, :]
        iq = iq_ref[r]
        iq_heads = jnp.concatenate(
            [iq[:, h * IDX_DIM:(h + 1) * IDX_DIM] for h in range(N_IDX_HEADS)], axis=0).astype(BF16)
        d = jnp.dot(iq_heads, ik_buf[slot, r].astype(BF16), preferred_element_type=F32)
        s = jnp.zeros((tn, total), F32)
        for h in range(N_IDX_HEADS):
            s = s + ikw[:, IDX_DIM + h:IDX_DIM + h + 1] * jnp.maximum(d[h * tn:(h + 1) * tn, :], 0.0)
        keys.append(s + 0.0)
    kpos = lax.broadcasted_iota(jnp.int32, (rows_all, total), 1)
    tok = lax.broadcasted_iota(jnp.int32, (rows_all, total), 0) % tn
    key = jnp.where(kpos <= past + tok, jnp.concatenate(keys, axis=0), -jnp.inf)

    def count(m):
        return jnp.sum(m.astype(jnp.int32), axis=1, keepdims=True)

    def count16(m):
        one = jnp.where(m, jnp.int16(1), jnp.int16(0))
        part = one[:, :LANES]
        for j in range(LANES, total, LANES):
            part = part + one[:, j:j + LANES]
        return jnp.sum(part.astype(jnp.int32), axis=1, keepdims=True)

    def kth16(arr, above):
        def body(it, t):
            w = lax.shift_left(jnp.int32(1), 14 - 2 * it)
            hits = [(above + count16(arr >= (t + k * w).astype(jnp.int16)) >= topk).astype(jnp.int32)
                    for k in (1, 2, 3)]
            return t + (hits[0] + hits[1] + hits[2]) * w
        return lax.fori_loop(0, 8, body, jnp.full((rows_all, 1), I16_MIN, jnp.int32))

    ikey = _f32_to_key(key)
    kh = lax.shift_right_arithmetic(ikey, 16).astype(jnp.int16)
    t_hi = kth16(kh, 0)
    h16 = t_hi.astype(jnp.int16)
    kl = jnp.where(kh == h16, ((ikey & 0xFFFF) + I16_MIN).astype(jnp.int16), jnp.int16(I16_MIN))
    guess = lax.shift_left(t_hi, 16) + (kth16(kl, count16(kh > h16)) - I16_MIN)
    live = past + lax.broadcasted_iota(jnp.int32, (rows_all, 1), 0) % tn + 1 >= topk
    thr = jnp.where(live, _key_to_f32(_confirm_kth(lambda c: count(key >= c), guess, live, topk)), -jnp.inf)
    n_gt = count(key > thr)
    tied = (key == thr) & live
    real = tok[:, :LANES] < t_new
    surplus = jnp.sum((real & (n_gt + count(tied) > topk)).astype(jnp.int32))
    cut = lax.cond(
        surplus > 0,
        lambda: _tie_cutoff(lambda x: count(tied & (kpos < x)), (rows_all, 1), topk - n_gt, n_bits),
        lambda: jnp.full((rows_all, 1), 2 ** 31 - 1, jnp.int32))
    bias = jnp.where((key > thr) | (tied & (kpos <= cut)), 0.0, -jnp.inf)

    for r in range(nr):
        for j in range(n_pages):
            for cp in kv_copies(step * nr + r, r, j):
                cp.wait()

    rep = N_HEADS_A // N_KV_A
    for r in range(nr):
        k_buf[r, :, past:] = tail_tile(kan_ref[r])
        v_buf[r, :, past:] = tail_tile(van_ref[r])
        qa = (qa_ref[r] * (HEAD_DIM_A ** -0.5)).astype(BF16)
        bias_g = jnp.concatenate([bias[r * tn:(r + 1) * tn]] * rep, axis=0)
        for g in range(N_KV_A):
            gs = slice(g * HEAD_DIM_A, (g + 1) * HEAD_DIM_A)
            qg = jnp.concatenate(
                [qa[:, (g * rep + i) * HEAD_DIM_A:(g * rep + i + 1) * HEAD_DIM_A] for i in range(rep)], axis=0)
            sc = jnp.dot(qg, k_buf[r, gs, :].astype(BF16), preferred_element_type=F32) + bias_g
            m = jnp.max(sc, axis=1, keepdims=True)
            p = jnp.exp(sc - m)
            og = lax.dot_general(p.astype(BF16), v_buf[r, gs, :].astype(BF16), (((1,), (1,)), ((), ())),
                                 preferred_element_type=F32) / jnp.sum(p, axis=1, keepdims=True)
            for i in range(rep):
                h = g * rep + i
                o_ref[r, :, h * HEAD_DIM_A:(h + 1) * HEAD_DIM_A] = og[i * tn:(i + 1) * tn, :]


def _dsa_sample(qa3, iq3, ikw3, ka3, va3, cache_idx_k, cache_k, cache_v, page_table, t_new):
    db, tn, _ = qa3.shape
    n_pages = page_table.shape[1]
    n_phys, page, _ = cache_idx_k.shape
    past = n_pages * page
    total = past + LANES
    topk = min(TOPK_MAX, (past + t_new) // 4)
    n_bits = max(1, int(math.ceil(math.log2(total))))
    nr = _pick_tile(db, 2)
    cik_t = jnp.transpose(cache_idx_k, (0, 2, 1))
    ck_t = jnp.transpose(cache_k, (0, 2, 3, 1)).reshape(n_phys, W_KV, page)
    cv_t = jnp.transpose(cache_v, (0, 2, 3, 1)).reshape(n_phys, W_KV, page)
    new = lambda w: pl.BlockSpec((nr, tn, w), lambda i, pt: (i, 0, 0))
    hbm = pl.BlockSpec(memory_space=pl.ANY)
    return pl.pallas_call(
        functools.partial(_dsa_sample_kernel, n_pages=n_pages, page=page, tn=tn, t_new=t_new, topk=topk,
                          n_bits=n_bits),
        grid_spec=pltpu.PrefetchScalarGridSpec(
            num_scalar_prefetch=1,
            grid=(db // nr,),
            in_specs=[new(W_A), new(W_IQ), new(LANES), new(W_KV), new(W_KV), hbm, hbm, hbm],
            out_specs=new(W_A),
            scratch_shapes=[pltpu.VMEM((2, nr, IDX_DIM, total), F32), pltpu.VMEM((nr, W_KV, total), F32),
                            pltpu.VMEM((nr, W_KV, total), F32), pltpu.SemaphoreType.DMA((4,))],
        ),
        out_shape=jax.ShapeDtypeStruct((db, tn, W_A), F32),
        compiler_params=pltpu.CompilerParams(
            dimension_semantics=("arbitrary",), vmem_limit_bytes=VMEM_LIMIT_BYTES),
        name="dsa_sample",
    )(page_table, qa3, iq3, ikw3, ka3, va3, cik_t, ck_t, cv_t)


def _pack_w_in(w_in):
    split = SEG_IKW[0] + IDX_DIM + N_IDX_HEADS
    pad = jnp.zeros((D_MODEL, _W_IK_PAD), w_in.dtype)
    return jnp.concatenate([w_in[:, :split], pad, w_in[:, split:]], axis=1).astype(BF16)


def _pick_tile(n, pref):
    t = min(n, pref)
    while n % t:
        t //= 2
    return t


def _group(x, pos, dsa_fn, mem_k3, mem_v3, shift_prev, wkv_prev, t_valid, wts):
    b, t, _ = x.shape
    n = b * t
    x2 = x.reshape(n, D_MODEL)
    tm = _pick_tile(n, 256)
    qa, ka, va, iq, ikw, pb, cq, gs = _inproj(x2, pos, wts["norm_mix_g"], wts["w_in"], wts["b_gate"], tm)
    r3 = lambda a: a.reshape(b, t, a.shape[-1])
    oa = dsa_fn(r3(qa), r3(iq), r3(ikw), r3(ka), r3(va))
    chunk = _pick_tile(t, RWKV_CHUNK)
    nb = _pick_tile(b, max(2, 32 // chunk))
    ob, wkv_new = _rwkv(r3(pb), shift_prev, wkv_prev, wts["rwkv"], chunk, nb, t_valid)
    tq = _pick_tile(t, 512)
    oc = _xattn(r3(cq), mem_k3, mem_v3, tq, _pick_tile(b, max(1, 64 // tq)))
    y = _merge_mlp(x2, oa.reshape(n, W_A), ob.reshape(n, W_B), oc.reshape(n, W_C), gs,
                   wts["w_branch_a"], wts["w_branch_b"], wts["w_branch_c"], wts["w_out"],
                   wts["norm_mlp_g"], wts["w_mlp_up"], wts["w_mlp_down"], wts["norm_final_g"], tm)
    return y.reshape(b, t, D_MODEL), r3(ka), r3(va), r3(ikw)[..., :IDX_DIM], wkv_new, r3(pb)


def kernel(x_prompt, x_sample, mem_prompt, cache_k, cache_v, cache_idx_k, cache_mem_k, cache_mem_v,
           state_wkv, state_shift, page_table, norm_mix_g, norm_mem_g, norm_mlp_g, norm_final_g,
           w_in, b_gate, w_mem_kv, rwkv_mu, rwkv_w0, rwkv_w_up, rwkv_a0, rwkv_a_up, rwkv_g_up,
           rwkv_k_k, rwkv_k_a, rwkv_r_k, rwkv_ln_g, rwkv_ln_b, w_branch_a, w_branch_b, w_branch_c,
           w_out, w_mlp_up, w_mlp_down):
    bf = lambda w: w.astype(BF16)
    wts = dict(
        norm_mix_g=norm_mix_g, norm_mlp_g=norm_mlp_g, norm_final_g=norm_final_g,
        w_in=_pack_w_in(w_in), b_gate=b_gate,
        rwkv=dict(mu=rwkv_mu, w0=rwkv_w0, w_up=rwkv_w_up, a0=rwkv_a0, a_up=rwkv_a_up, g_up=rwkv_g_up,
                  k_k=rwkv_k_k, k_a=rwkv_k_a, r_k=rwkv_r_k, ln_g=rwkv_ln_g, ln_b=rwkv_ln_b),
        w_branch_a=bf(w_branch_a), w_branch_b=bf(w_branch_b), w_branch_c=bf(w_branch_c),
        w_out=bf(w_out), w_mlp_up=bf(w_mlp_up), w_mlp_down=bf(w_mlp_down))

    b, t, _ = x_prompt.shape
    n_mem = mem_prompt.shape[1]
    mem2 = mem_prompt.reshape(b * n_mem, D_MODEL)
    mk, mv = _memkv(mem2, norm_mem_g, bf(w_mem_kv), _pick_tile(b * n_mem, 256))
    mk3, mv3 = (a.reshape(b, n_mem * N_HEADS_C, HEAD_DIM_C) for a in (mk, mv))
    shift0 = jnp.zeros((b, RWKV_PROJ), F32)
    wkv0 = jnp.zeros((b, N_HEADS_B, HEAD_DIM_B, HEAD_DIM_B), F32)
    dsa_p = functools.partial(_dsa_prompt, tq=_pick_tile(t, 256))
    y_p, k_p, v_p, ik_p, wkv_p, pb_p = _group(
        x_prompt, jnp.arange(t), dsa_p, mk3, mv3, shift0, wkv0, None, wts)

    db, tn, _ = x_sample.shape
    n_pages = page_table.shape[1]
    page = cache_idx_k.shape[1]
    past = n_pages * page
    tp = -(-tn // SUBLANES) * SUBLANES
    x_s = jnp.pad(x_sample, ((0, 0), (0, tp - tn), (0, 0)))
    dsa_s = functools.partial(_dsa_sample, cache_idx_k=cache_idx_k, cache_k=cache_k, cache_v=cache_v,
                              page_table=page_table, t_new=tn)
    y_s, k_s, v_s, ik_s, wkv_s, pb_s = _group(
        x_s, past + jnp.arange(tp), dsa_s, cache_mem_k.reshape(db, n_mem * N_HEADS_C, HEAD_DIM_C),
        cache_mem_v.reshape(db, n_mem * N_HEADS_C, HEAD_DIM_C), state_shift, state_wkv, tn, wts)

    heads = lambda a: a.reshape(a.shape[0], a.shape[1], N_KV_A, HEAD_DIM_A)
    memh = lambda a: a.reshape(b, n_mem, N_HEADS_C, HEAD_DIM_C)
    return (y_p, y_s[:, :tn], heads(k_p), heads(v_p), ik_p, memh(mk3), memh(mv3), wkv_p, pb_p[:, -1],
            heads(k_s[:, :tn]), heads(v_s[:, :tn]), ik_s[:, :tn], wkv_s, pb_s[:, tn - 1])
```

```python
import functools
import math

import jax
import jax.numpy as jnp
from jax import lax
from jax.experimental import pallas as pl
from jax.experimental.pallas import tpu as pltpu

F32 = jnp.float32
BF16 = jnp.bfloat16

D_MODEL = 1024
N_HEADS_A, N_KV_A, HEAD_DIM_A = 8, 2, 64
W_A = N_HEADS_A * HEAD_DIM_A
W_KV = N_KV_A * HEAD_DIM_A
N_IDX_HEADS, IDX_DIM = 4, 64
W_IQ = N_IDX_HEADS * IDX_DIM
TOPK_MAX = 256
ROPE_THETA = 500000.0
ROT_DIM = 16
N_HEADS_B, HEAD_DIM_B = 8, 64
W_B = N_HEADS_B * HEAD_DIM_B
D_DECAY_LORA, D_AAA_LORA, D_GATE_LORA = 64, 64, 128
RWKV_PROJ = 3 * W_B + D_DECAY_LORA + D_AAA_LORA + D_GATE_LORA
GN_EPS = 64e-5
N_HEADS_C, HEAD_DIM_C = 4, 128
W_C = N_HEADS_C * HEAD_DIM_C
N_BRANCH = 3
D_FF = 4 * D_MODEL
RMS_EPS = 1e-6

LANES = 128
SUBLANES = 8
VMEM_LIMIT_BYTES = 56 * 1024 * 1024

INT_MIN = -(2 ** 31)
I16_MIN = -(2 ** 15)

RWKV_CHUNK = 128

_W_IK_PAD = LANES - IDX_DIM - N_IDX_HEADS
SEG_QA = (0, W_A)
SEG_KA = (W_A, W_KV)
SEG_VA = (W_A + W_KV, W_KV)
SEG_IQ = (W_A + 2 * W_KV, W_IQ)
SEG_IKW = (SEG_IQ[0] + W_IQ, LANES)
SEG_PB = (SEG_IKW[0] + LANES, RWKV_PROJ)
SEG_CQ = (SEG_PB[0] + RWKV_PROJ, W_C)
SEG_G = (SEG_CQ[0] + W_C, N_BRANCH * D_MODEL)
W_PACKED = SEG_G[0] + SEG_G[1]


def _rmsnorm(x, g):
    ms = jnp.mean(x * x, axis=-1, keepdims=True)
    return x * lax.rsqrt(ms + RMS_EPS) * g


def _rope(y, cos, s_lo, s_hi):
    return y * cos + pltpu.roll(y, LANES - ROT_DIM // 2, axis=1) * s_lo + pltpu.roll(y, ROT_DIM // 2, axis=1) * s_hi


def _inproj_kernel(x_ref, g_ref, w_ref, bg_ref, cos_ref, slo_ref, shi_ref,
                   qa_ref, ka_ref, va_ref, iq_ref, ikw_ref, pb_ref, cq_ref, gs_ref):
    xn = _rmsnorm(x_ref[...], g_ref[...]).astype(BF16)
    cos, s_lo, s_hi = cos_ref[...], slo_ref[...], shi_ref[...]

    def proj(off, width):
        return jnp.dot(xn, w_ref[:, off:off + width], preferred_element_type=F32)

    def rope_into(out_ref, c, y):
        for j in range(0, y.shape[1], LANES):
            out_ref[:, c + j:c + j + LANES] = _rope(y[:, j:j + LANES], cos, s_lo, s_hi)

    for c in range(0, SEG_QA[1], 2 * LANES):
        rope_into(qa_ref, c, proj(SEG_QA[0] + c, 2 * LANES))
    kv = proj(SEG_KA[0], SEG_KA[1] + SEG_VA[1])
    rope_into(ka_ref, 0, kv[:, :SEG_KA[1]])
    va_ref[...] = kv[:, SEG_KA[1]:]
    rope_into(iq_ref, 0, proj(*SEG_IQ))
    y = proj(*SEG_IKW)
    lane = lax.broadcasted_iota(jnp.int32, y.shape, 1)
    ikw_ref[...] = jnp.where(lane < IDX_DIM, _rope(y, cos, s_lo, s_hi), y * (W_IQ ** -0.5))
    for c in range(0, SEG_PB[1], 256):
        pb_ref[:, c:c + 256] = proj(SEG_PB[0] + c, 256)
    cq_ref[...] = proj(*SEG_CQ)
    for c in range(0, SEG_G[1], 512):
        gs_ref[:, c:c + 512] = jax.nn.sigmoid(proj(SEG_G[0] + c, 512) + bg_ref[:, c:c + 512])


def _rope_tables(pos, rows):
    half = ROT_DIM // 2
    inv_freq = ROPE_THETA ** (-jnp.arange(half, dtype=F32) / half)
    ang = pos.astype(F32)[:, None] * inv_freq[None, :]
    cos, sin = jnp.cos(ang), jnp.sin(ang)
    t = pos.shape[0]
    ones = jnp.ones((t, HEAD_DIM_A - ROT_DIM), F32)
    zeros_h = jnp.zeros((t, half), F32)
    zeros_r = jnp.zeros((t, HEAD_DIM_A - ROT_DIM), F32)
    c64 = jnp.concatenate([cos, cos, ones], axis=1)
    lo64 = jnp.concatenate([-sin, zeros_h, zeros_r], axis=1)
    hi64 = jnp.concatenate([zeros_h, sin, zeros_r], axis=1)
    reps = rows // t
    return tuple(jnp.tile(jnp.concatenate([a, a], axis=1), (reps, 1)) for a in (c64, lo64, hi64))


def _inproj(x2, pos, norm_g, w_packed, b_gate, tm):
    n = x2.shape[0]
    t = pos.shape[0]
    rows = max(t, tm)
    cos, s_lo, s_hi = _rope_tables(pos, rows)
    nt = rows // tm
    row = lambda i: (i, 0)
    const = lambda i: (0, 0)
    tab = pl.BlockSpec((tm, LANES), lambda i: (i % nt, 0))
    widths = [SEG_QA[1], SEG_KA[1], SEG_VA[1], SEG_IQ[1], SEG_IKW[1], SEG_PB[1], SEG_CQ[1], SEG_G[1]]
    return pl.pallas_call(
        _inproj_kernel,
        grid=(n // tm,),
        in_specs=[
            pl.BlockSpec((tm, D_MODEL), row),
            pl.BlockSpec((1, D_MODEL), const),
            pl.BlockSpec((D_MODEL, W_PACKED), const, pipeline_mode=pl.Buffered(1)),
            pl.BlockSpec((1, SEG_G[1]), const),
            tab, tab, tab,
        ],
        out_specs=[pl.BlockSpec((tm, w), row) for w in widths],
        out_shape=[jax.ShapeDtypeStruct((n, w), F32) for w in widths],
        compiler_params=pltpu.CompilerParams(
            dimension_semantics=("arbitrary",), vmem_limit_bytes=VMEM_LIMIT_BYTES),
        name="inproj",
    )(x2, norm_g.reshape(1, D_MODEL), w_packed, b_gate.reshape(1, -1), cos, s_lo, s_hi)


def _memkv_kernel(x_ref, g_ref, w_ref, k_ref, v_ref):
    xn = _rmsnorm(x_ref[...], g_ref[...]).astype(BF16)
    tm = x_ref.shape[0]
    for out_ref, off in ((k_ref, 0), (v_ref, W_C)):
        y = jnp.dot(xn, w_ref[:, off:off + W_C], preferred_element_type=F32)
        for h in range(N_HEADS_C):
            out_ref[pl.ds(h, tm, stride=N_HEADS_C), :] = y[:, h * HEAD_DIM_C:(h + 1) * HEAD_DIM_C]


def _memkv(mem2, norm_g, w_bf16, tm):
    n = mem2.shape[0]
    row = lambda i: (i, 0)
    const = lambda i: (0, 0)
    return pl.pallas_call(
        _memkv_kernel,
        grid=(n // tm,),
        in_specs=[pl.BlockSpec((tm, D_MODEL), row), pl.BlockSpec((1, D_MODEL), const),
                  pl.BlockSpec((D_MODEL, 2 * W_C), const)],
        out_specs=[pl.BlockSpec((tm * N_HEADS_C, HEAD_DIM_C), row)] * 2,
        out_shape=[jax.ShapeDtypeStruct((n * N_HEADS_C, HEAD_DIM_C), F32)] * 2,
        compiler_params=pltpu.CompilerParams(dimension_semantics=("arbitrary",)),
        name="memkv",
    )(mem2, norm_g.reshape(1, D_MODEL), w_bf16)


def _xattn_kernel(q_ref, k_ref, v_ref, o_ref):
    n_mem = k_ref.shape[1] // N_HEADS_C
    units = [(g, h) for g in range(q_ref.shape[0]) for h in range(N_HEADS_C)]
    head = lambda ref, g, h: ref[g, pl.ds(h, n_mem, stride=N_HEADS_C), :].astype(BF16)
    cols = lambda h: slice(h * HEAD_DIM_C, (h + 1) * HEAD_DIM_C)
    s = [lax.dot_general(q_ref[g, :, cols(h)].astype(BF16), head(k_ref, g, h), (((1,), (1,)), ((), ())),
                         preferred_element_type=F32) * (HEAD_DIM_C ** -0.5) for g, h in units]
    p = []
    for sj in s:
        e = jnp.exp(sj - jnp.max(sj, axis=-1, keepdims=True))
        p.append((e / jnp.sum(e, axis=-1, keepdims=True)).astype(BF16))
    for pj, (g, h) in zip(p, units):
        o_ref[g, :, cols(h)] = jnp.dot(pj, head(v_ref, g, h), preferred_element_type=F32)


def _xattn(cq3, mk3, mv3, tq, nb):
    b, t, _ = cq3.shape
    rows = mk3.shape[1]
    return pl.pallas_call(
        _xattn_kernel,
        grid=(b // nb, t // tq),
        in_specs=[pl.BlockSpec((nb, tq, W_C), lambda i, j: (i, j, 0)),
                  pl.BlockSpec((nb, rows, HEAD_DIM_C), lambda i, j: (i, 0, 0)),
                  pl.BlockSpec((nb, rows, HEAD_DIM_C), lambda i, j: (i, 0, 0))],
        out_specs=pl.BlockSpec((nb, tq, W_C), lambda i, j: (i, j, 0)),
        out_shape=jax.ShapeDtypeStruct((b, t, W_C), F32),
        compiler_params=pltpu.CompilerParams(dimension_semantics=("arbitrary", "arbitrary")),
        name="xattn",
    )(cq3, mk3, mv3)


def _merge_mlp_kernel(x_ref, oa_ref, ob_ref, oc_ref, gs_ref, wa_ref, wb_ref, wc_ref, wo_ref,
                      gm_ref, wu_ref, wd_ref, gf_ref, y_ref):
    def bdot(a, w):
        return jnp.dot(a.astype(BF16), w, preferred_element_type=F32)

    merged = (gs_ref[:, :D_MODEL] * bdot(oa_ref[...], wa_ref[...])
              + gs_ref[:, D_MODEL:2 * D_MODEL] * bdot(ob_ref[...], wb_ref[...])
              + gs_ref[:, 2 * D_MODEL:] * bdot(oc_ref[...], wc_ref[...]))
    h = x_ref[...] + bdot(merged, wo_ref[...])
    hn = _rmsnorm(h, gm_ref[...]).astype(BF16)
    acc = h
    for c in range(0, D_FF, 1024):
        u = jnp.dot(hn, wu_ref[:, c:c + 1024], preferred_element_type=F32)
        acc = acc + bdot(jnp.square(jnp.maximum(u, 0.0)), wd_ref[c:c + 1024, :])
    y_ref[...] = _rmsnorm(acc, gf_ref[...])


def _merge_mlp(x2, oa, ob, oc, gs, wa, wb, wc, wo, g_mlp, wu, wd, g_final, tm):
    n = x2.shape[0]
    row = lambda i: (i, 0)
    const = lambda i: (0, 0)
    resident = lambda shape: pl.BlockSpec(shape, const, pipeline_mode=pl.Buffered(1))
    return pl.pallas_call(
        _merge_mlp_kernel,
        grid=(n // tm,),
        in_specs=[
            pl.BlockSpec((tm, D_MODEL), row),
            pl.BlockSpec((tm, W_A), row), pl.BlockSpec((tm, W_B), row), pl.BlockSpec((tm, W_C), row),
            pl.BlockSpec((tm, N_BRANCH * D_MODEL), row),
            resident((W_A, D_MODEL)), resident((W_B, D_MODEL)), resident((W_C, D_MODEL)),
            resident((D_MODEL, D_MODEL)),
            pl.BlockSpec((1, D_MODEL), const),
            resident((D_MODEL, D_FF)), resident((D_FF, D_MODEL)),
            pl.BlockSpec((1, D_MODEL), const),
        ],
        out_specs=pl.BlockSpec((tm, D_MODEL), row),
        out_shape=jax.ShapeDtypeStruct((n, D_MODEL), F32),
        compiler_params=pltpu.CompilerParams(
            dimension_semantics=("arbitrary",), vmem_limit_bytes=VMEM_LIMIT_BYTES),
        name="merge_mlp",
    )(x2, oa, ob, oc, gs, wa, wb, wc, wo, g_mlp.reshape(1, -1), wu, wd, g_final.reshape(1, -1))


def _bf16_terms(x, n):
    terms = []
    for _ in range(n):
        t = x.astype(BF16)
        terms.append(t)
        x = x - t.astype(F32)
    return terms


def _dot_split(a, b_terms):
    (ah, al), (bh, bl) = _bf16_terms(a, 2), b_terms
    d = lambda x, y: jnp.dot(x, y, preferred_element_type=F32)
    return d(ah, bh) + (d(ah, bl) + d(al, bh))


def _head_sums(x):
    lo = lax.broadcasted_iota(jnp.int32, (x.shape[0], LANES), 1) < HEAD_DIM_B
    out = []
    for c in range(0, x.shape[1], LANES):
        xc = x[:, c:c + LANES]
        s_lo = jnp.sum(jnp.where(lo, xc, 0.0), axis=-1, keepdims=True)
        s_hi = jnp.sum(jnp.where(lo, 0.0, xc), axis=-1, keepdims=True)
        out.append(jnp.where(lo, s_lo, s_hi))
    return jnp.concatenate(out, axis=1)


def _rwkv_kernel(pb_ref, sh_ref, s0_ref, mu_ref, w0_ref, wup_ref, a0_ref, aup_ref, gup_ref,
                 kk_ref, ka_ref, rk_ref, lng_ref, lnb_ref, ob_ref, sout_ref, carry_ref, state_ref,
                 *, chunk, t_valid):
    c = pl.program_id(1)
    nb = pb_ref.shape[0]

    @pl.when(c == 0)
    def _():
        carry_ref[...] = sh_ref[...]
        state_ref[...] = s0_ref[...]

    row1 = lax.broadcasted_iota(jnp.int32, (chunk, 1), 0)
    rr = lax.broadcasted_iota(jnp.int32, (chunk, chunk), 0)
    cc = lax.broadcasted_iota(jnp.int32, (chunk, chunk), 1)
    strict, incl = rr > cc, rr >= cc
    tri = incl.astype(BF16)
    w_up, a_up, g_up = (_bf16_terms(w[...], 2) for w in (wup_ref, aup_ref, gup_ref))

    def prepare(bi):
        pbc = pb_ref[bi]
        prev = jnp.where(row1 == 0, carry_ref[bi], pltpu.roll(pbc, 1, axis=0))
        carry_ref[bi] = pbc[chunk - 1:chunk, :]
        ps = pbc + (prev - pbc) * mu_ref[...]
        r, k, v = ps[:, :W_B], ps[:, W_B:2 * W_B], ps[:, 2 * W_B:3 * W_B]
        o = 3 * W_B
        wl = ps[:, o:o + D_DECAY_LORA]
        al = ps[:, o + D_DECAY_LORA:o + D_DECAY_LORA + D_AAA_LORA]
        gl = ps[:, o + D_DECAY_LORA + D_AAA_LORA:]
        z = -(w0_ref[...] + _dot_split(jnp.tanh(wl), w_up))
        softplus = jnp.maximum(z, 0.0) + jnp.log(1.0 + jnp.exp(-jnp.abs(z)))
        ld = -jnp.exp(-softplus - 0.5)
        alpha = jax.nn.sigmoid(a0_ref[...] + _dot_split(al, a_up))
        gate = _dot_split(jax.nn.sigmoid(gl), g_up)
        kkf = k * kk_ref[...]
        khf = k * (1.0 + (alpha - 1.0) * ka_ref[...])
        if t_valid is not None:
            valid = (row1 + c * chunk) < t_valid
            ld = jnp.where(valid, ld, 0.0)
            alpha = jnp.where(valid, alpha, 0.0)
            khf = jnp.where(valid, khf, 0.0)
        cum = sum(jnp.dot(tri, part, preferred_element_type=F32) for part in _bf16_terms(ld, 3))
        ecum, einv, eprev = jnp.exp(cum), jnp.exp(-cum), jnp.exp(cum - ld)
        kkn = kkf * lax.rsqrt(jnp.maximum(_head_sums(kkf * kkf), 1e-24))
        b_t = kkn * alpha * einv
        k_t = khf * einv
        g_c = ecum[chunk - 1:chunk, :]
        return dict(r=r, v=v, khf=khf, gate=gate, g_c=g_c,
                    a_t=(-kkn * eprev).astype(BF16), r_t=(r * ecum).astype(BF16),
                    b_c=(b_t * g_c).astype(BF16), k_c=(k_t * g_c).astype(BF16),
                    b_t=b_t.astype(BF16), k_t=k_t.astype(BF16), vb=v.astype(BF16))

    def mm(a, b):
        return jnp.dot(a.astype(BF16), b.astype(BF16), preferred_element_type=F32)

    def mm_t(a, b):
        return lax.dot_general(a.astype(BF16), b.astype(BF16), (((1,), (1,)), ((), ())),
                               preferred_element_type=F32)

    rows = [prepare(bi) for bi in range(nb)]
    n_it = max(1, int(math.ceil(math.log2(chunk))))
    units = [(bi, h, slice(h * HEAD_DIM_B, (h + 1) * HEAD_DIM_B))
             for bi in range(nb) for h in range(N_HEADS_B)]
    idx = range(len(units))
    col = lambda name: [rows[bi][name][:, sl] for bi, _, sl in units]
    a_t, r_t, b_t, k_t, b_c, k_c, vb = (col(n) for n in ("a_t", "r_t", "b_t", "k_t", "b_c", "k_c", "vb"))
    x1 = [jnp.concatenate([a_t[j], r_t[j]], axis=0) for j in idx]
    s0 = [state_ref[bi, h] for bi, h, _ in units]
    p0 = [mm_t(x1[j], s0[j]) for j in idx]
    if chunk % LANES == 0:
        g = [mm_t(x1[j], jnp.concatenate([b_t[j], k_t[j]], axis=0)) for j in idx]
        incl2 = jnp.concatenate([incl, incl], axis=1)
        u = [p0[j][:chunk] + mm(jnp.where(strict, g[j][:chunk, chunk:], 0.0), vb[j]) for j in idx]
        lp = [jnp.where(strict, g[j][:chunk, :chunk], 0.0).astype(BF16) for j in idx]
        for it in range(n_it - 1):
            sq = [mm(lp[j], jnp.concatenate([lp[j], u[j].astype(BF16)], axis=1)) for j in idx]
            u = [u[j] + sq[j][:, chunk:] for j in idx]
            lp = [sq[j][:, :chunk].astype(BF16) for j in idx]
        u = [u[j] + mm(lp[j], u[j]) for j in idx]
        uv = [jnp.concatenate([u[j].astype(BF16), vb[j]], axis=0) for j in idx]
        y = [p0[j][chunk:] + mm(jnp.where(incl2, g[j][chunk:], 0.0), uv[j]) for j in idx]
    else:
        g_b = [mm_t(x1[j], b_t[j]) for j in idx]
        g_k = [mm_t(x1[j], k_t[j]) for j in idx]
        u = [p0[j][:chunk] + mm(jnp.where(strict, g_k[j][:chunk], 0.0), vb[j]) for j in idx]
        lp = [jnp.where(strict, g_b[j][:chunk], 0.0).astype(BF16) for j in idx]
        for it in range(n_it):
            u = [u[j] + mm(lp[j], u[j]) for j in idx]
            if it + 1 < n_it:
                lp = [mm(lp[j], lp[j]).astype(BF16) for j in idx]
        y = [p0[j][chunk:] + mm(jnp.where(incl, g_b[j][chunk:], 0.0), u[j])
             + mm(jnp.where(incl, g_k[j][chunk:], 0.0), vb[j]) for j in idx]
        uv = [jnp.concatenate([u[j].astype(BF16), vb[j]], axis=0) for j in idx]
    for j, (bi, h, sl) in enumerate(units):
        bk = jnp.concatenate([b_c[j], k_c[j]], axis=0)
        state_ref[bi, h] = s0[j] * rows[bi]["g_c"][:, sl] + lax.dot_general(
            uv[j], bk, (((0,), (0,)), ((), ())), preferred_element_type=F32)

    for bi, row in enumerate(rows):
        y_all = jnp.concatenate(y[bi * N_HEADS_B:(bi + 1) * N_HEADS_B], axis=1)
        d = y_all - _head_sums(y_all) * (1.0 / HEAD_DIM_B)
        var = _head_sums(d * d) * (1.0 / HEAD_DIM_B)
        yn = d * lax.rsqrt(var + GN_EPS) * lng_ref[...] + lnb_ref[...]
        bonus = _head_sums(row["r"] * row["khf"] * rk_ref[...]) * row["v"]
        ob_ref[bi] = (yn + bonus) * row["gate"]

    @pl.when(c == pl.num_programs(1) - 1)
    def _():
        sout_ref[...] = state_ref[...]


def _rwkv(pb3, shift_prev, wkv_prev, p, chunk, nb, t_valid):
    b, t, _ = pb3.shape
    const = lambda i, j: (0, 0)
    vec = lambda n: pl.BlockSpec((1, n), const)
    state_spec = pl.BlockSpec((nb, N_HEADS_B, HEAD_DIM_B, HEAD_DIM_B), lambda i, j: (i, 0, 0, 0))
    return pl.pallas_call(
        functools.partial(_rwkv_kernel, chunk=chunk, t_valid=t_valid),
        grid=(b // nb, t // chunk),
        in_specs=[
            pl.BlockSpec((nb, chunk, RWKV_PROJ), lambda i, j: (i, j, 0)),
            pl.BlockSpec((nb, 1, RWKV_PROJ), lambda i, j: (i, 0, 0)),
            state_spec,
            vec(RWKV_PROJ), vec(W_B), pl.BlockSpec((D_DECAY_LORA, W_B), const),
            vec(W_B), pl.BlockSpec((D_AAA_LORA, W_B), const), pl.BlockSpec((D_GATE_LORA, W_B), const),
            vec(W_B), vec(W_B), vec(W_B), vec(W_B), vec(W_B),
        ],
        out_specs=[pl.BlockSpec((nb, chunk, W_B), lambda i, j: (i, j, 0)), state_spec],
        out_shape=[jax.ShapeDtypeStruct((b, t, W_B), F32),
                   jax.ShapeDtypeStruct(wkv_prev.shape, F32)],
        scratch_shapes=[pltpu.VMEM((nb, 1, RWKV_PROJ), F32),
                        pltpu.VMEM((nb, N_HEADS_B, HEAD_DIM_B, HEAD_DIM_B), F32)],
        compiler_params=pltpu.CompilerParams(dimension_semantics=("arbitrary", "arbitrary")),
        name="rwkv",
    )(pb3, shift_prev.reshape(b, 1, RWKV_PROJ), wkv_prev,
      p["mu"].reshape(1, -1), p["w0"].reshape(1, -1), p["w_up"], p["a0"].reshape(1, -1), p["a_up"],
      p["g_up"], p["k_k"].reshape(1, -1), p["k_a"].reshape(1, -1), p["r_k"].reshape(1, -1),
      p["ln_g"].reshape(1, -1), p["ln_b"].reshape(1, -1))


def _f32_to_key(s):
    b = lax.bitcast_convert_type(s, jnp.int32)
    return b ^ (lax.shift_right_arithmetic(b, 31) & 0x7FFFFFFF)


def _key_to_f32(k):
    return lax.bitcast_convert_type(k ^ (lax.shift_right_arithmetic(k, 31) & 0x7FFFFFFF), F32)


def _next_key(k):
    n = k + 1
    n = jnp.where((n >= -(1 << 23)) & (n < 0), 0, n)
    return jnp.where((n > 0) & (n < (1 << 23)), 1 << 23, n)


def _confirm_kth(count_ge, guess, live, topk):
    def n_true(m):
        return jnp.sum((m & live).astype(jnp.int32))

    def widen(c):
        lo, hi, step, _ = c
        low = count_ge(_key_to_f32(lo)) < topk
        high = count_ge(_key_to_f32(hi)) >= topk
        return (jnp.where(low, lo - step, lo), jnp.where(high, hi + step, hi), step * 2,
                n_true(low | high))

    def halve(c):
        lo, hi, _ = c
        mid = lo + lax.shift_right_arithmetic(hi - lo, 1)
        ok = count_ge(_key_to_f32(mid)) >= topk
        is_open = hi > _next_key(lo)
        lo, hi = jnp.where(is_open & ok, mid, lo), jnp.where(is_open & ~ok, mid, hi)
        return lo, hi, n_true(hi > _next_key(lo))

    lo, hi, _, _ = lax.while_loop(lambda c: c[3] > 0, widen,
                                  (guess, _next_key(guess), jnp.int32(1 << 12), jnp.int32(1)))
    lo, _, _ = lax.while_loop(lambda c: c[2] > 0, halve, (lo, hi, n_true(hi > _next_key(lo))))
    return lo


def _tie_cutoff(count_eq_before, shape, need, n_bits):
    def body(it, x):
        cand = x + lax.shift_left(jnp.int32(1), n_bits - 1 - it)
        return jnp.where(count_eq_before(cand) < need, cand, x)
    return lax.fori_loop(0, n_bits, body, jnp.zeros(shape, jnp.int32))


def _dsa_prompt_kernel(qa_ref, iq_ref, ikwq_ref, ka_ref, va_ref, ikw_ref, o_ref,
                       sc_ref, kh_ref, kl_ref, cnt_ref, cnt16_ref, tied_ref, m_ref, l_ref, acc_ref, *, tq, tk, topk):
    i = pl.program_id(1)
    n_full = ((i + 1) * tq) // tk
    has_part = ((i + 1) * tq) % tk != 0
    iq = iq_ref[0]
    iq_heads = jnp.concatenate(
        [iq[:, h * IDX_DIM:(h + 1) * IDX_DIM] for h in range(N_IDX_HEADS)], axis=0).astype(BF16)
    w_t = ikwq_ref[0].T
    sizes = sorted({tk, tq})
    kpos0 = {n: lax.broadcasted_iota(jnp.int32, (n, tq), 0) for n in sizes}
    qpos = {n: i * tq + lax.broadcasted_iota(jnp.int32, (n, tq), 1) for n in sizes}
    dn_t = (((1,), (1,)), ((), ()))
    dn_0 = (((0,), (0,)), ((), ()))

    def for_tiles(body, init, carry_ref=None):
        def full(kt, c):
            return body(pl.ds(pl.multiple_of(kt * tk, tk), tk), tk, c)
        c = lax.fori_loop(0, n_full, full, init)
        if tk == tq:
            return c
        last = pl.ds(pl.multiple_of(n_full * tk, tq), tq)
        if carry_ref is None:
            @pl.when(has_part)
            def _():
                body(last, tq, c)
            return c
        carry_ref[...] = c

        @pl.when(has_part)
        def _():
            carry_ref[...] = body(last, tq, carry_ref[...])
        return carry_ref[...]

    def scores(rows, n, carry):
        ik = ikw_ref[0, rows, :][:, :IDX_DIM].astype(BF16)
        d = lax.dot_general(ik, iq_heads, dn_t, preferred_element_type=F32)
        s = jnp.zeros((n, tq), F32)
        for h in range(N_IDX_HEADS):
            s = s + w_t[IDX_DIM + h:IDX_DIM + h + 1, :] * jnp.maximum(d[:, h * tq:(h + 1) * tq], 0.0)
        sc = jnp.where(kpos0[n] + rows.start <= qpos[n], s + 0.0, -jnp.inf)
        sc_ref[rows, :] = sc
        key = _f32_to_key(sc)
        kh_ref[rows, :] = lax.shift_right_arithmetic(key, 16).astype(jnp.int16)
        kl_ref[rows, :] = ((key & 0xFFFF) + I16_MIN).astype(jnp.int16)
        return carry

    for_tiles(scores, 0)

    def count(pred):
        def body(rows, n, acc):
            m = pred(sc_ref[rows, :]).astype(jnp.int32)
            return acc + jnp.sum(m.reshape(n // SUBLANES, SUBLANES, tq), axis=0)
        acc = for_tiles(body, jnp.zeros((SUBLANES, tq), jnp.int32), cnt_ref)
        return jnp.sum(acc, axis=0, keepdims=True)

    rows16 = 2 * SUBLANES

    def count16(ref, pred):
        def body(rows, n, acc):
            one = jnp.where(pred(ref[rows, :]), jnp.int16(1), jnp.int16(0))
            part = one[0:rows16, :]
            for j in range(rows16, n, rows16):
                part = part + one[j:j + rows16, :]
            return acc + part.astype(jnp.int32)
        acc = for_tiles(body, jnp.zeros((rows16, tq), jnp.int32), cnt16_ref)
        return jnp.sum(acc, axis=0, keepdims=True)

    def kth16(ref, above):
        def body(it, t):
            cand = t + lax.shift_left(jnp.int32(1), 15 - it)
            c16 = cand.astype(jnp.int16)
            return jnp.where(above + count16(ref, lambda blk: blk >= c16) >= topk, cand, t)
        return lax.fori_loop(0, 16, body, jnp.full((1, tq), I16_MIN, jnp.int32))

    t_hi = kth16(kh_ref, 0)
    h16 = t_hi.astype(jnp.int16)
    n_hi = count16(kh_ref, lambda blk: blk > h16)

    def keep_equal(rows, n, carry):
        kl_ref[rows, :] = jnp.where(kh_ref[rows, :] == h16, kl_ref[rows, :], jnp.int16(I16_MIN))
        return carry

    for_tiles(keep_equal, 0)
    guess = lax.shift_left(t_hi, 16) + (kth16(kl_ref, n_hi) - I16_MIN)
    live = i * tq + lax.broadcasted_iota(jnp.int32, (1, tq), 1) + 1 >= topk
    k_thr = _confirm_kth(lambda c: count(lambda blk: blk >= c), guess, live, topk)
    thr = jnp.where(live, _key_to_f32(k_thr), -jnp.inf)
    need = (topk - count(lambda blk: blk > thr)).astype(F32)
    before = {n: (lax.broadcasted_iota(jnp.int32, (n, n), 0)
                  > lax.broadcasted_iota(jnp.int32, (n, n), 1)).astype(BF16) for n in sizes}
    ones = {n: jnp.ones((n, SUBLANES), BF16) for n in sizes}

    def select(rows, n, tied_before):
        blk = sc_ref[rows, :]
        tied = (blk == thr) & live
        tied_b = jnp.where(tied, 1.0, 0.0).astype(BF16)
        rank = tied_before + jnp.dot(before[n], tied_b, preferred_element_type=F32)
        sel = (blk > thr) | (tied & (rank < need))
        sc_ref[rows, :] = jnp.where(sel, 0.0, -jnp.inf)
        return tied_before + lax.dot_general(ones[n], tied_b, dn_0, preferred_element_type=F32)[:1, :]

    for_tiles(select, jnp.zeros((1, tq), F32), tied_ref)

    m_ref[...] = jnp.full(m_ref.shape, -jnp.inf, F32)
    l_ref[...] = jnp.zeros(l_ref.shape, F32)
    acc_ref[...] = jnp.zeros(acc_ref.shape, F32)
    qa = (qa_ref[0] * (HEAD_DIM_A ** -0.5 * math.log2(math.e))).astype(BF16)
    heads = range(N_HEADS_A)
    low_lanes = {n: lax.broadcasted_iota(jnp.int32, (n, W_KV), 1) < HEAD_DIM_A for n in sizes}
    hs = [slice(h * HEAD_DIM_A, (h + 1) * HEAD_DIM_A) for h in heads]
    gs = [slice((h // (N_HEADS_A // N_KV_A)) * HEAD_DIM_A, (h // (N_HEADS_A // N_KV_A) + 1) * HEAD_DIM_A)
          for h in heads]

    def attend(rows, n, carry):
        bias = sc_ref[rows, :]
        kk = ka_ref[0, rows, :].astype(BF16)
        v32 = va_ref[0, rows, :]
        v_aug = [jnp.where(low_lanes[n], v32 if g == 0 else pltpu.roll(v32, HEAD_DIM_A, axis=1), 1.0)
                 .astype(BF16) for g in range(N_KV_A)]
        s = [lax.dot_general(kk[:, gs[h]], qa[:, hs[h]], dn_t, preferred_element_type=F32) + bias
             for h in heads]
        for h in heads:
            m_old = m_ref[h:h + 1, :]
            m_new = jnp.maximum(m_old, jnp.max(s[h], axis=0, keepdims=True))
            m_safe = jnp.where(m_new == -jnp.inf, 0.0, m_new)
            alpha = jnp.exp2(m_old - m_safe)
            p = jnp.exp2((s[h] - m_safe).astype(BF16))
            m_ref[h:h + 1, :] = m_new
            pv = lax.dot_general(v_aug[h // (N_HEADS_A // N_KV_A)], p, dn_0, preferred_element_type=F32)
            l_ref[h:h + 1, :] = alpha * l_ref[h:h + 1, :] + pv[HEAD_DIM_A:HEAD_DIM_A + 1, :]
            acc_ref[hs[h], :] = alpha * acc_ref[hs[h], :] + pv[:HEAD_DIM_A, :]
        return carry

    for_tiles(attend, 0)
    for h in heads:
        acc_ref[hs[h], :] = acc_ref[hs[h], :] / l_ref[h:h + 1, :]
    o_ref[0] = acc_ref[...].T


def _dsa_prompt(qa3, iq3, ikw3, ka3, va3, tq):
    b, t, _ = qa3.shape
    topk = min(TOPK_MAX, t // 4)
    assert tq >= topk and t % tq == 0
    tk = 2 * tq if t % (2 * tq) == 0 else tq
    qtile = lambda w: pl.BlockSpec((1, tq, w), lambda i, j: (i, j, 0))
    whole = lambda w: pl.BlockSpec((1, t, w), lambda i, j: (i, 0, 0))
    return pl.pallas_call(
        functools.partial(_dsa_prompt_kernel, tq=tq, tk=tk, topk=topk),
        grid=(b, t // tq),
        in_specs=[qtile(W_A), qtile(W_IQ), qtile(LANES), whole(W_KV), whole(W_KV), whole(LANES)],
        out_specs=qtile(W_A),
        out_shape=jax.ShapeDtypeStruct((b, t, W_A), F32),
        scratch_shapes=[pltpu.VMEM((t, tq), F32),
                        pltpu.VMEM((t, tq), jnp.int16), pltpu.VMEM((t, tq), jnp.int16),
                        pltpu.VMEM((SUBLANES, tq), jnp.int32), pltpu.VMEM((2 * SUBLANES, tq), jnp.int32),
                        pltpu.VMEM((1, tq), F32),
                        pltpu.VMEM((N_HEADS_A, tq), F32), pltpu.VMEM((N_HEADS_A, tq), F32),
                        pltpu.VMEM((W_A, tq), F32)],
        compiler_params=pltpu.CompilerParams(
            dimension_semantics=("arbitrary", "arbitrary"), vmem_limit_bytes=VMEM_LIMIT_BYTES),
        name="dsa_prompt",
    )(qa3, iq3, ikw3, ka3, va3, ikw3)


def _dsa_sample_kernel(pt_ref, qa_ref, iq_ref, ikwn_ref, kan_ref, van_ref, cik_ref, ck_ref, cv_ref,
                       o_ref, ik_buf, k_buf, v_buf, sems, *, n_pages, page, tn, t_new, topk, n_bits):
    step = pl.program_id(0)
    n_steps = pl.num_programs(0)
    nr = qa_ref.shape[0]
    past = n_pages * page
    total = past + LANES
    slot = step % 2
    sem_k, sem_v = 2, 3

    def ik_copy(row, dst, r, j):
        return pltpu.make_async_copy(cik_ref.at[pt_ref[row, j]],
                                     ik_buf.at[dst, r, :, pl.ds(j * page, page)], sems.at[dst])

    def kv_copies(row, r, j):
        pg, cols = pt_ref[row, j], pl.ds(j * page, page)
        return (pltpu.make_async_copy(ck_ref.at[pg], k_buf.at[r, :, cols], sems.at[sem_k]),
                pltpu.make_async_copy(cv_ref.at[pg], v_buf.at[r, :, cols], sems.at[sem_v]))

    def start_ik(at_step, dst):
        for r in range(nr):
            for j in range(n_pages):
                ik_copy(at_step * nr + r, dst, r, j).start()

    @pl.when(step == 0)
    def _():
        start_ik(0, 0)

    for r in range(nr):
        for j in range(n_pages):
            for cp in kv_copies(step * nr + r, r, j):
                cp.start()

    @pl.when(step + 1 < n_steps)
    def _():
        start_ik(step + 1, 1 - slot)

    for r in range(nr):
        for j in range(n_pages):
            ik_copy(step * nr + r, slot, r, j).wait()

    def tail_tile(x):
        return jnp.concatenate([x, jnp.zeros((LANES - tn, x.shape[1]), F32)], axis=0).T

    rows_all = nr * tn
    keys = []
    for r in range(nr):
        ikw = ikwn_ref[r]
        ik_buf[slot, r, :, past:] = tail_tile(ikw)[:IDX_DIM, :]
        iq = iq_ref[r]
        iq_heads = jnp.concatenate(
            [iq[:, h * IDX_DIM:(h + 1) * IDX_DIM] for h in range(N_IDX_HEADS)], axis=0).astype(BF16)
        d = jnp.dot(iq_heads, ik_buf[slot, r].astype(BF16), preferred_element_type=F32)
        s = jnp.zeros((tn, total), F32)
        for h in range(N_IDX_HEADS):
            s = s + ikw[:, IDX_DIM + h:IDX_DIM + h + 1] * jnp.maximum(d[h * tn:(h + 1) * tn, :], 0.0)
        keys.append(s + 0.0)
    kpos = lax.broadcasted_iota(jnp.int32, (rows_all, total), 1)
    tok = lax.broadcasted_iota(jnp.int32, (rows_all, total), 0) % tn
    key = jnp.where(kpos <= past + tok, jnp.concatenate(keys, axis=0), -jnp.inf)

    def count(m):
        return jnp.sum(m.astype(jnp.int32), axis=1, keepdims=True)

    def count16(m):
        one = jnp.where(m, jnp.int16(1), jnp.int16(0))
        part = one[:, :LANES]
        for j in range(LANES, total, LANES):
            part = part + one[:, j:j + LANES]
        return jnp.sum(part.astype(jnp.int32), axis=1, keepdims=True)

    def kth16(arr, above):
        def body(it, t):
            w = lax.shift_left(jnp.int32(1), 14 - 2 * it)
            hits = [(above + count16(arr >= (t + k * w).astype(jnp.int16)) >= topk).astype(jnp.int32)
                    for k in (1, 2, 3)]
            return t + (hits[0] + hits[1] + hits[2]) * w
        return lax.fori_loop(0, 8, body, jnp.full((rows_all, 1), I16_MIN, jnp.int32))

    ikey = _f32_to_key(key)
    kh = lax.shift_right_arithmetic(ikey, 16).astype(jnp.int16)
    t_hi = kth16(kh, 0)
    h16 = t_hi.astype(jnp.int16)
    kl = jnp.where(kh == h16, ((ikey & 0xFFFF) + I16_MIN).astype(jnp.int16), jnp.int16(I16_MIN))
    guess = lax.shift_left(t_hi, 16) + (kth16(kl, count16(kh > h16)) - I16_MIN)
    live = past + lax.broadcasted_iota(jnp.int32, (rows_all, 1), 0) % tn + 1 >= topk
    thr = jnp.where(live, _key_to_f32(_confirm_kth(lambda c: count(key >= c), guess, live, topk)), -jnp.inf)
    n_gt = count(key > thr)
    tied = (key == thr) & live
    real = tok[:, :LANES] < t_new
    surplus = jnp.sum((real & (n_gt + count(tied) > topk)).astype(jnp.int32))
    cut = lax.cond(
        surplus > 0,
        lambda: _tie_cutoff(lambda x: count(tied & (kpos < x)), (rows_all, 1), topk - n_gt, n_bits),
        lambda: jnp.full((rows_all, 1), 2 ** 31 - 1, jnp.int32))
    bias = jnp.where((key > thr) | (tied & (kpos <= cut)), 0.0, -jnp.inf)

    for r in range(nr):
        for j in range(n_pages):
            for cp in kv_copies(step * nr + r, r, j):
                cp.wait()

    rep = N_HEADS_A // N_KV_A
    for r in range(nr):
        k_buf[r, :, past:] = tail_tile(kan_ref[r])
        v_buf[r, :, past:] = tail_tile(van_ref[r])
        qa = (qa_ref[r] * (HEAD_DIM_A ** -0.5)).astype(BF16)
        bias_g = jnp.concatenate([bias[r * tn:(r + 1) * tn]] * rep, axis=0)
        for g in range(N_KV_A):
            gs = slice(g * HEAD_DIM_A, (g + 1) * HEAD_DIM_A)
            qg = jnp.concatenate(
                [qa[:, (g * rep + i) * HEAD_DIM_A:(g * rep + i + 1) * HEAD_DIM_A] for i in range(rep)], axis=0)
            sc = jnp.dot(qg, k_buf[r, gs, :].astype(BF16), preferred_element_type=F32) + bias_g
            m = jnp.max(sc, axis=1, keepdims=True)
            p = jnp.exp(sc - m)
            og = lax.dot_general(p.astype(BF16), v_buf[r, gs, :].astype(BF16), (((1,), (1,)), ((), ())),
                                 preferred_element_type=F32) / jnp.sum(p, axis=1, keepdims=True)
            for i in range(rep):
                h = g * rep + i
                o_ref[r, :, h * HEAD_DIM_A:(h + 1) * HEAD_DIM_A] = og[i * tn:(i + 1) * tn, :]


def _dsa_sample(qa3, iq3, ikw3, ka3, va3, cache_idx_k, cache_k, cache_v, page_table, t_new):
    db, tn, _ = qa3.shape
    n_pages = page_table.shape[1]
    n_phys, page, _ = cache_idx_k.shape
    past = n_pages * page
    total = past + LANES
    topk = min(TOPK_MAX, (past + t_new) // 4)
    n_bits = max(1, int(math.ceil(math.log2(total))))
    nr = _pick_tile(db, 2)
    cik_t = jnp.transpose(cache_idx_k, (0, 2, 1))
    ck_t = jnp.transpose(cache_k, (0, 2, 3, 1)).reshape(n_phys, W_KV, page)
    cv_t = jnp.transpose(cache_v, (0, 2, 3, 1)).reshape(n_phys, W_KV, page)
    new = lambda w: pl.BlockSpec((nr, tn, w), lambda i, pt: (i, 0, 0))
    hbm = pl.BlockSpec(memory_space=pl.ANY)
    return pl.pallas_call(
        functools.partial(_dsa_sample_kernel, n_pages=n_pages, page=page, tn=tn, t_new=t_new, topk=topk,
                          n_bits=n_bits),
        grid_spec=pltpu.PrefetchScalarGridSpec(
            num_scalar_prefetch=1,
            grid=(db // nr,),
            in_specs=[new(W_A), new(W_IQ), new(LANES), new(W_KV), new(W_KV), hbm, hbm, hbm],
            out_specs=new(W_A),
            scratch_shapes=[pltpu.VMEM((2, nr, IDX_DIM, total), F32), pltpu.VMEM((nr, W_KV, total), F32),
                            pltpu.VMEM((nr, W_KV, total), F32), pltpu.SemaphoreType.DMA((4,))],
        ),
        out_shape=jax.ShapeDtypeStruct((db, tn, W_A), F32),
        compiler_params=pltpu.CompilerParams(
            dimension_semantics=("arbitrary",), vmem_limit_bytes=VMEM_LIMIT_BYTES),
        name="dsa_sample",
    )(page_table, qa3, iq3, ikw3, ka3, va3, cik_t, ck_t, cv_t)


def _pack_w_in(w_in):
    split = SEG_IKW[0] + IDX_DIM + N_IDX_HEADS
    pad = jnp.zeros((D_MODEL, _W_IK_PAD), w_in.dtype)
    return jnp.concatenate([w_in[:, :split], pad, w_in[:, split:]], axis=1).astype(BF16)


def _pick_tile(n, pref):
    t = min(n, pref)
    while n % t:
        t //= 2
    return t


def _group(x, pos, dsa_fn, mem_k3, mem_v3, shift_prev, wkv_prev, t_valid, wts):
    b, t, _ = x.shape
    n = b * t
    x2 = x.reshape(n, D_MODEL)
    tm = _pick_tile(n, 256)
    qa, ka, va, iq, ikw, pb, cq, gs = _inproj(x2, pos, wts["norm_mix_g"], wts["w_in"], wts["b_gate"], tm)
    r3 = lambda a: a.reshape(b, t, a.shape[-1])
    oa = dsa_fn(r3(qa), r3(iq), r3(ikw), r3(ka), r3(va))
    chunk = _pick_tile(t, RWKV_CHUNK)
    nb = _pick_tile(b, max(2, 32 // chunk))
    ob, wkv_new = _rwkv(r3(pb), shift_prev, wkv_prev, wts["rwkv"], chunk, nb, t_valid)
    tq = _pick_tile(t, 512)
    oc = _xattn(r3(cq), mem_k3, mem_v3, tq, _pick_tile(b, max(1, 64 // tq)))
    y = _merge_mlp(x2, oa.reshape(n, W_A), ob.reshape(n, W_B), oc.reshape(n, W_C), gs,
                   wts["w_branch_a"], wts["w_branch_b"], wts["w_branch_c"], wts["w_out"],
                   wts["norm_mlp_g"], wts["w_mlp_up"], wts["w_mlp_down"], wts["norm_final_g"], tm)
    return y.reshape(b, t, D_MODEL), r3(ka), r3(va), r3(ikw)[..., :IDX_DIM], wkv_new, r3(pb)


def kernel(x_prompt, x_sample, mem_prompt, cache_k, cache_v, cache_idx_k, cache_mem_k, cache_mem_v,
           state_wkv, state_shift, page_table, norm_mix_g, norm_mem_g, norm_mlp_g, norm_final_g,
           w_in, b_gate, w_mem_kv, rwkv_mu, rwkv_w0, rwkv_w_up, rwkv_a0, rwkv_a_up, rwkv_g_up,
           rwkv_k_k, rwkv_k_a, rwkv_r_k, rwkv_ln_g, rwkv_ln_b, w_branch_a, w_branch_b, w_branch_c,
           w_out, w_mlp_up, w_mlp_down):
    bf = lambda w: w.astype(BF16)
    wts = dict(
        norm_mix_g=norm_mix_g, norm_mlp_g=norm_mlp_g, norm_final_g=norm_final_g,
        w_in=_pack_w_in(w_in), b_gate=b_gate,
        rwkv=dict(mu=rwkv_mu, w0=rwkv_w0, w_up=rwkv_w_up, a0=rwkv_a0, a_up=rwkv_a_up, g_up=rwkv_g_up,
                  k_k=rwkv_k_k, k_a=rwkv_k_a, r_k=rwkv_r_k, ln_g=rwkv_ln_g, ln_b=rwkv_ln_b),
        w_branch_a=bf(w_branch_a), w_branch_b=bf(w_branch_b), w_branch_c=bf(w_branch_c),
        w_out=bf(w_out), w_mlp_up=bf(w_mlp_up), w_mlp_down=bf(w_mlp_down))

    b, t, _ = x_prompt.shape
    n_mem = mem_prompt.shape[1]
    mem2 = mem_prompt.reshape(b * n_mem, D_MODEL)
    mk, mv = _memkv(mem2, norm_mem_g, bf(w_mem_kv), _pick_tile(b * n_mem, 256))
    mk3, mv3 = (a.reshape(b, n_mem * N_HEADS_C, HEAD_DIM_C) for a in (mk, mv))
    shift0 = jnp.zeros((b, RWKV_PROJ), F32)
    wkv0 = jnp.zeros((b, N_HEADS_B, HEAD_DIM_B, HEAD_DIM_B), F32)
    dsa_p = functools.partial(_dsa_prompt, tq=_pick_tile(t, 256))
    y_p, k_p, v_p, ik_p, wkv_p, pb_p = _group(
        x_prompt, jnp.arange(t), dsa_p, mk3, mv3, shift0, wkv0, None, wts)

    db, tn, _ = x_sample.shape
    n_pages = page_table.shape[1]
    page = cache_idx_k.shape[1]
    past = n_pages * page
    tp = -(-tn // SUBLANES) * SUBLANES
    x_s = jnp.pad(x_sample, ((0, 0), (0, tp - tn), (0, 0)))
    dsa_s = functools.partial(_dsa_sample, cache_idx_k=cache_idx_k, cache_k=cache_k, cache_v=cache_v,
                              page_table=page_table, t_new=tn)
    y_s, k_s, v_s, ik_s, wkv_s, pb_s = _group(
        x_s, past + jnp.arange(tp), dsa_s, cache_mem_k.reshape(db, n_mem * N_HEADS_C, HEAD_DIM_C),
        cache_mem_v.reshape(db, n_mem * N_HEADS_C, HEAD_DIM_C), state_shift, state_wkv, tn, wts)

    heads = lambda a: a.reshape(a.shape[0], a.shape[1], N_KV_A, HEAD_DIM_A)
    memh = lambda a: a.reshape(b, n_mem, N_HEADS_C, HEAD_DIM_C)
    return (y_p, y_s[:, :tn], heads(k_p), heads(v_p), ik_p, memh(mk3), memh(mv3), wkv_p, pb_p[:, -1],
            heads(k_s[:, :tn]), heads(v_s[:, :tn]), ik_s[:, :tn], wkv_s, pb_s[:, tn - 1])
```

```python
import functools
import math

import jax
import jax.numpy as jnp
from jax import lax
from jax.experimental import pallas as pl
from jax.experimental.pallas import tpu as pltpu

F32 = jnp.float32
BF16 = jnp.bfloat16

D_MODEL = 1024
N_HEADS_A, N_KV_A, HEAD_DIM_A = 8, 2, 64
W_A = N_HEADS_A * HEAD_DIM_A
W_KV = N_KV_A * HEAD_DIM_A
N_IDX_HEADS, IDX_DIM = 4, 64
W_IQ = N_IDX_HEADS * IDX_DIM
TOPK_MAX = 256
ROPE_THETA = 500000.0
ROT_DIM = 16
N_HEADS_B, HEAD_DIM_B = 8, 64
W_B = N_HEADS_B * HEAD_DIM_B
D_DECAY_LORA, D_AAA_LORA, D_GATE_LORA = 64, 64, 128
RWKV_PROJ = 3 * W_B + D_DECAY_LORA + D_AAA_LORA + D_GATE_LORA
GN_EPS = 64e-5
N_HEADS_C, HEAD_DIM_C = 4, 128
W_C = N_HEADS_C * HEAD_DIM_C
N_BRANCH = 3
D_FF = 4 * D_MODEL
RMS_EPS = 1e-6

LANES = 128
SUBLANES = 8
VMEM_LIMIT_BYTES = 56 * 1024 * 1024

INT_MIN = -(2 ** 31)
I16_MIN = -(2 ** 15)

RWKV_CHUNK = 128

_W_IK_PAD = LANES - IDX_DIM - N_IDX_HEADS
SEG_QA = (0, W_A)
SEG_KA = (W_A, W_KV)
SEG_VA = (W_A + W_KV, W_KV)
SEG_IQ = (W_A + 2 * W_KV, W_IQ)
SEG_IKW = (SEG_IQ[0] + W_IQ, LANES)
SEG_PB = (SEG_IKW[0] + LANES, RWKV_PROJ)
SEG_CQ = (SEG_PB[0] + RWKV_PROJ, W_C)
SEG_G = (SEG_CQ[0] + W_C, N_BRANCH * D_MODEL)
W_PACKED = SEG_G[0] + SEG_G[1]


def _rmsnorm(x, g):
    ms = jnp.mean(x * x, axis=-1, keepdims=True)
    return x * lax.rsqrt(ms + RMS_EPS) * g


def _rope(y, cos, s_lo, s_hi):
    return y * cos + pltpu.roll(y, LANES - ROT_DIM // 2, axis=1) * s_lo + pltpu.roll(y, ROT_DIM // 2, axis=1) * s_hi


def _inproj_kernel(x_ref, g_ref, w_ref, bg_ref, cos_ref, slo_ref, shi_ref,
                   qa_ref, ka_ref, va_ref, iq_ref, ikw_ref, pb_ref, cq_ref, gs_ref):
    xn = _rmsnorm(x_ref[...], g_ref[...]).astype(BF16)
    cos, s_lo, s_hi = cos_ref[...], slo_ref[...], shi_ref[...]

    def proj(off, width):
        return jnp.dot(xn, w_ref[:, off:off + width], preferred_element_type=F32)

    def rope_into(out_ref, c, y):
        for j in range(0, y.shape[1], LANES):
            out_ref[:, c + j:c + j + LANES] = _rope(y[:, j:j + LANES], cos, s_lo, s_hi)

    for c in range(0, SEG_QA[1], 2 * LANES):
        rope_into(qa_ref, c, proj(SEG_QA[0] + c, 2 * LANES))
    kv = proj(SEG_KA[0], SEG_KA[1] + SEG_VA[1])
    rope_into(ka_ref, 0, kv[:, :SEG_KA[1]])
    va_ref[...] = kv[:, SEG_KA[1]:]
    rope_into(iq_ref, 0, proj(*SEG_IQ))
    y = proj(*SEG_IKW)
    lane = lax.broadcasted_iota(jnp.int32, y.shape, 1)
    ikw_ref[...] = jnp.where(lane < IDX_DIM, _rope(y, cos, s_lo, s_hi), y * (W_IQ ** -0.5))
    for c in range(0, SEG_PB[1], 256):
        pb_ref[:, c:c + 256] = proj(SEG_PB[0] + c, 256)
    cq_ref[...] = proj(*SEG_CQ)
    for c in range(0, SEG_G[1], 512):
        gs_ref[:, c:c + 512] = jax.nn.sigmoid(proj(SEG_G[0] + c, 512) + bg_ref[:, c:c + 512])


def _rope_tables(pos, rows):
    half = ROT_DIM // 2
    inv_freq = ROPE_THETA ** (-jnp.arange(half, dtype=F32) / half)
    ang = pos.astype(F32)[:, None] * inv_freq[None, :]
    cos, sin = jnp.cos(ang), jnp.sin(ang)
    t = pos.shape[0]
    ones = jnp.ones((t, HEAD_DIM_A - ROT_DIM), F32)
    zeros_h = jnp.zeros((t, half), F32)
    zeros_r = jnp.zeros((t, HEAD_DIM_A - ROT_DIM), F32)
    c64 = jnp.concatenate([cos, cos, ones], axis=1)
    lo64 = jnp.concatenate([-sin, zeros_h, zeros_r], axis=1)
    hi64 = jnp.concatenate([zeros_h, sin, zeros_r], axis=1)
    reps = rows // t
    return tuple(jnp.tile(jnp.concatenate([a, a], axis=1), (reps, 1)) for a in (c64, lo64, hi64))


def _inproj(x2, pos, norm_g, w_packed, b_gate, tm):
    n = x2.shape[0]
    t = pos.shape[0]
    rows = max(t, tm)
    cos, s_lo, s_hi = _rope_tables(pos, rows)
    nt = rows // tm
    row = lambda i: (i, 0)
    const = lambda i: (0, 0)
    tab = pl.BlockSpec((tm, LANES), lambda i: (i % nt, 0))
    widths = [SEG_QA[1], SEG_KA[1], SEG_VA[1], SEG_IQ[1], SEG_IKW[1], SEG_PB[1], SEG_CQ[1], SEG_G[1]]
    return pl.pallas_call(
        _inproj_kernel,
        grid=(n // tm,),
        in_specs=[
            pl.BlockSpec((tm, D_MODEL), row),
            pl.BlockSpec((1, D_MODEL), const),
            pl.BlockSpec((D_MODEL, W_PACKED), const, pipeline_mode=pl.Buffered(1)),
            pl.BlockSpec((1, SEG_G[1]), const),
            tab, tab, tab,
        ],
        out_specs=[pl.BlockSpec((tm, w), row) for w in widths],
        out_shape=[jax.ShapeDtypeStruct((n, w), F32) for w in widths],
        compiler_params=pltpu.CompilerParams(
            dimension_semantics=("arbitrary",), vmem_limit_bytes=VMEM_LIMIT_BYTES),
        name="inproj",
    )(x2, norm_g.reshape(1, D_MODEL), w_packed, b_gate.reshape(1, -1), cos, s_lo, s_hi)


def _memkv_kernel(x_ref, g_ref, w_ref, k_ref, v_ref):
    xn = _rmsnorm(x_ref[...], g_ref[...]).astype(BF16)
    tm = x_ref.shape[0]
    for out_ref, off in ((k_ref, 0), (v_ref, W_C)):
        y = jnp.dot(xn, w_ref[:, off:off + W_C], preferred_element_type=F32)
        for h in range(N_HEADS_C):
            out_ref[pl.ds(h, tm, stride=N_HEADS_C), :] = y[:, h * HEAD_DIM_C:(h + 1) * HEAD_DIM_C]


def _memkv(mem2, norm_g, w_bf16, tm):
    n = mem2.shape[0]
    row = lambda i: (i, 0)
    const = lambda i: (0, 0)
    return pl.pallas_call(
        _memkv_kernel,
        grid=(n // tm,),
        in_specs=[pl.BlockSpec((tm, D_MODEL), row), pl.BlockSpec((1, D_MODEL), const),
                  pl.BlockSpec((D_MODEL, 2 * W_C), const)],
        out_specs=[pl.BlockSpec((tm * N_HEADS_C, HEAD_DIM_C), row)] * 2,
        out_shape=[jax.ShapeDtypeStruct((n * N_HEADS_C, HEAD_DIM_C), F32)] * 2,
        compiler_params=pltpu.CompilerParams(dimension_semantics=("arbitrary",)),
        name="memkv",
    )(mem2, norm_g.reshape(1, D_MODEL), w_bf16)


def _xattn_kernel(q_ref, k_ref, v_ref, o_ref):
    n_mem = k_ref.shape[1] // N_HEADS_C
    units = [(g, h) for g in range(q_ref.shape[0]) for h in range(N_HEADS_C)]
    head = lambda ref, g, h: ref[g, pl.ds(h, n_mem, stride=N_HEADS_C), :].astype(BF16)
    cols = lambda h: slice(h * HEAD_DIM_C, (h + 1) * HEAD_DIM_C)
    s = [lax.dot_general(q_ref[g, :, cols(h)].astype(BF16), head(k_ref, g, h), (((1,), (1,)), ((), ())),
                         preferred_element_type=F32) * (HEAD_DIM_C ** -0.5) for g, h in units]
    p = []
    for sj in s:
        e = jnp.exp(sj - jnp.max(sj, axis=-1, keepdims=True))
        p.append((e / jnp.sum(e, axis=-1, keepdims=True)).astype(BF16))
    for pj, (g, h) in zip(p, units):
        o_ref[g, :, cols(h)] = jnp.dot(pj, head(v_ref, g, h), preferred_element_type=F32)


def _xattn(cq3, mk3, mv3, tq, nb):
    b, t, _ = cq3.shape
    rows = mk3.shape[1]
    return pl.pallas_call(
        _xattn_kernel,
        grid=(b // nb, t // tq),
        in_specs=[pl.BlockSpec((nb, tq, W_C), lambda i, j: (i, j, 0)),
                  pl.BlockSpec((nb, rows, HEAD_DIM_C), lambda i, j: (i, 0, 0)),
                  pl.BlockSpec((nb, rows, HEAD_DIM_C), lambda i, j: (i, 0, 0))],
        out_specs=pl.BlockSpec((nb, tq, W_C), lambda i, j: (i, j, 0)),
        out_shape=jax.ShapeDtypeStruct((b, t, W_C), F32),
        compiler_params=pltpu.CompilerParams(dimension_semantics=("arbitrary", "arbitrary")),
        name="xattn",
    )(cq3, mk3, mv3)


def _merge_mlp_kernel(x_ref, oa_ref, ob_ref, oc_ref, gs_ref, wa_ref, wb_ref, wc_ref, wo_ref,
                      gm_ref, wu_ref, wd_ref, gf_ref, y_ref):
    def bdot(a, w):
        return jnp.dot(a.astype(BF16), w, preferred_element_type=F32)

    merged = (gs_ref[:, :D_MODEL] * bdot(oa_ref[...], wa_ref[...])
              + gs_ref[:, D_MODEL:2 * D_MODEL] * bdot(ob_ref[...], wb_ref[...])
              + gs_ref[:, 2 * D_MODEL:] * bdot(oc_ref[...], wc_ref[...]))
    h = x_ref[...] + bdot(merged, wo_ref[...])
    hn = _rmsnorm(h, gm_ref[...]).astype(BF16)
    acc = h
    for c in range(0, D_FF, 1024):
        u = jnp.dot(hn, wu_ref[:, c:c + 1024], preferred_element_type=F32)
        acc = acc + bdot(jnp.square(jnp.maximum(u, 0.0)), wd_ref[c:c + 1024, :])
    y_ref[...] = _rmsnorm(acc, gf_ref[...])


def _merge_mlp(x2, oa, ob, oc, gs, wa, wb, wc, wo, g_mlp, wu, wd, g_final, tm):
    n = x2.shape[0]
    row = lambda i: (i, 0)
    const = lambda i: (0, 0)
    resident = lambda shape: pl.BlockSpec(shape, const, pipeline_mode=pl.Buffered(1))
    return pl.pallas_call(
        _merge_mlp_kernel,
        grid=(n // tm,),
        in_specs=[
            pl.BlockSpec((tm, D_MODEL), row),
            pl.BlockSpec((tm, W_A), row), pl.BlockSpec((tm, W_B), row), pl.BlockSpec((tm, W_C), row),
            pl.BlockSpec((tm, N_BRANCH * D_MODEL), row),
            resident((W_A, D_MODEL)), resident((W_B, D_MODEL)), resident((W_C, D_MODEL)),
            resident((D_MODEL, D_MODEL)),
            pl.BlockSpec((1, D_MODEL), const),
            resident((D_MODEL, D_FF)), resident((D_FF, D_MODEL)),
            pl.BlockSpec((1, D_MODEL), const),
        ],
        out_specs=pl.BlockSpec((tm, D_MODEL), row),
        out_shape=jax.ShapeDtypeStruct((n, D_MODEL), F32),
        compiler_params=pltpu.CompilerParams(
            dimension_semantics=("arbitrary",), vmem_limit_bytes=VMEM_LIMIT_BYTES),
        name="merge_mlp",
    )(x2, oa, ob, oc, gs, wa, wb, wc, wo, g_mlp.reshape(1, -1), wu, wd, g_final.reshape(1, -1))


def _bf16_terms(x, n):
    terms = []
    for _ in range(n):
        t = x.astype(BF16)
        terms.append(t)
        x = x - t.astype(F32)
    return terms


def _dot_split(a, b_terms):
    (ah, al), (bh, bl) = _bf16_terms(a, 2), b_terms
    d = lambda x, y: jnp.dot(x, y, preferred_element_type=F32)
    return d(ah, bh) + (d(ah, bl) + d(al, bh))


def _head_sums(x):
    lo = lax.broadcasted_iota(jnp.int32, (x.shape[0], LANES), 1) < HEAD_DIM_B
    out = []
    for c in range(0, x.shape[1], LANES):
        xc = x[:, c:c + LANES]
        s_lo = jnp.sum(jnp.where(lo, xc, 0.0), axis=-1, keepdims=True)
        s_hi = jnp.sum(jnp.where(lo, 0.0, xc), axis=-1, keepdims=True)
        out.append(jnp.where(lo, s_lo, s_hi))
    return jnp.concatenate(out, axis=1)


def _rwkv_kernel(pb_ref, sh_ref, s0_ref, mu_ref, w0_ref, wup_ref, a0_ref, aup_ref, gup_ref,
                 kk_ref, ka_ref, rk_ref, lng_ref, lnb_ref, ob_ref, sout_ref, carry_ref, state_ref,
                 *, chunk, t_valid):
    c = pl.program_id(1)
    nb = pb_ref.shape[0]

    @pl.when(c == 0)
    def _():
        carry_ref[...] = sh_ref[...]
        state_ref[...] = s0_ref[...]

    row1 = lax.broadcasted_iota(jnp.int32, (chunk, 1), 0)
    rr = lax.broadcasted_iota(jnp.int32, (chunk, chunk), 0)
    cc = lax.broadcasted_iota(jnp.int32, (chunk, chunk), 1)
    strict, incl = rr > cc, rr >= cc
    tri = incl.astype(BF16)
    w_up, a_up, g_up = (_bf16_terms(w[...], 2) for w in (wup_ref, aup_ref, gup_ref))

    def prepare(bi):
        pbc = pb_ref[bi]
        prev = jnp.where(row1 == 0, carry_ref[bi], pltpu.roll(pbc, 1, axis=0))
        carry_ref[bi] = pbc[chunk - 1:chunk, :]
        ps = pbc + (prev - pbc) * mu_ref[...]
        r, k, v = ps[:, :W_B], ps[:, W_B:2 * W_B], ps[:, 2 * W_B:3 * W_B]
        o = 3 * W_B
        wl = ps[:, o:o + D_DECAY_LORA]
        al = ps[:, o + D_DECAY_LORA:o + D_DECAY_LORA + D_AAA_LORA]
        gl = ps[:, o + D_DECAY_LORA + D_AAA_LORA:]
        z = -(w0_ref[...] + _dot_split(jnp.tanh(wl), w_up))
        softplus = jnp.maximum(z, 0.0) + jnp.log(1.0 + jnp.exp(-jnp.abs(z)))
        ld = -jnp.exp(-softplus - 0.5)
        alpha = jax.nn.sigmoid(a0_ref[...] + _dot_split(al, a_up))
        gate = _dot_split(jax.nn.sigmoid(gl), g_up)
        kkf = k * kk_ref[...]
        khf = k * (1.0 + (alpha - 1.0) * ka_ref[...])
        if t_valid is not None:
            valid = (row1 + c * chunk) < t_valid
            ld = jnp.where(valid, ld, 0.0)
            alpha = jnp.where(valid, alpha, 0.0)
            khf = jnp.where(valid, khf, 0.0)
        cum = sum(jnp.dot(tri, part, preferred_element_type=F32) for part in _bf16_terms(ld, 3))
        ecum, einv, eprev = jnp.exp(cum), jnp.exp(-cum), jnp.exp(cum - ld)
        kkn = kkf * lax.rsqrt(jnp.maximum(_head_sums(kkf * kkf), 1e-24))
        b_t = kkn * alpha * einv
        k_t = khf * einv
        g_c = ecum[chunk - 1:chunk, :]
        return dict(r=r, v=v, khf=khf, gate=gate, g_c=g_c,
                    a_t=(-kkn * eprev).astype(BF16), r_t=(r * ecum).astype(BF16),
                    b_c=(b_t * g_c).astype(BF16), k_c=(k_t * g_c).astype(BF16),
                    b_t=b_t.astype(BF16), k_t=k_t.astype(BF16), vb=v.astype(BF16))

    def mm(a, b):
        return jnp.dot(a.astype(BF16), b.astype(BF16), preferred_element_type=F32)

    def mm_t(a, b):
        return lax.dot_general(a.astype(BF16), b.astype(BF16), (((1,), (1,)), ((), ())),
                               preferred_element_type=F32)

    rows = [prepare(bi) for bi in range(nb)]
    n_it = max(1, int(math.ceil(math.log2(chunk))))
    units = [(bi, h, slice(h * HEAD_DIM_B, (h + 1) * HEAD_DIM_B))
             for bi in range(nb) for h in range(N_HEADS_B)]
    idx = range(len(units))
    col = lambda name: [rows[bi][name][:, sl] for bi, _, sl in units]
    a_t, r_t, b_t, k_t, b_c, k_c, vb = (col(n) for n in ("a_t", "r_t", "b_t", "k_t", "b_c", "k_c", "vb"))
    x1 = [jnp.concatenate([a_t[j], r_t[j]], axis=0) for j in idx]
    s0 = [state_ref[bi, h] for bi, h, _ in units]
    p0 = [mm_t(x1[j], s0[j]) for j in idx]
    if chunk % LANES == 0:
        g = [mm_t(x1[j], jnp.concatenate([b_t[j], k_t[j]], axis=0)) for j in idx]
        incl2 = jnp.concatenate([incl, incl], axis=1)
        u = [p0[j][:chunk] + mm(jnp.where(strict, g[j][:chunk, chunk:], 0.0), vb[j]) for j in idx]
        lp = [jnp.where(strict, g[j][:chunk, :chunk], 0.0).astype(BF16) for j in idx]
        for it in range(n_it - 1):
            sq = [mm(lp[j], jnp.concatenate([lp[j], u[j].astype(BF16)], axis=1)) for j in idx]
            u = [u[j] + sq[j][:, chunk:] for j in idx]
            lp = [sq[j][:, :chunk].astype(BF16) for j in idx]
        u = [u[j] + mm(lp[j], u[j]) for j in idx]
        uv = [jnp.concatenate([u[j].astype(BF16), vb[j]], axis=0) for j in idx]
        y = [p0[j][chunk:] + mm(jnp.where(incl2, g[j][chunk:], 0.0), uv[j]) for j in idx]
    else:
        g_b = [mm_t(x1[j], b_t[j]) for j in idx]
        g_k = [mm_t(x1[j], k_t[j]) for j in idx]
        u = [p0[j][:chunk] + mm(jnp.where(strict, g_k[j][:chunk], 0.0), vb[j]) for j in idx]
        lp = [jnp.where(strict, g_b[j][:chunk], 0.0).astype(BF16) for j in idx]
        for it in range(n_it):
            u = [u[j] + mm(lp[j], u[j]) for j in idx]
            if it + 1 < n_it:
                lp = [mm(lp[j], lp[j]).astype(BF16) for j in idx]
        y = [p0[j][chunk:] + mm(jnp.where(incl, g_b[j][chunk:], 0.0), u[j])
             + mm(jnp.where(incl, g_k[j][chunk:], 0.0), vb[j]) for j in idx]
        uv = [jnp.concatenate([u[j].astype(BF16), vb[j]], axis=0) for j in idx]
    for j, (bi, h, sl) in enumerate(units):
        bk = jnp.concatenate([b_c[j], k_c[j]], axis=0)
        state_ref[bi, h] = s0[j] * rows[bi]["g_c"][:, sl] + lax.dot_general(
            uv[j], bk, (((0,), (0,)), ((), ())), preferred_element_type=F32)

    for bi, row in enumerate(rows):
        y_all = jnp.concatenate(y[bi * N_HEADS_B:(bi + 1) * N_HEADS_B], axis=1)
        d = y_all - _head_sums(y_all) * (1.0 / HEAD_DIM_B)
        var = _head_sums(d * d) * (1.0 / HEAD_DIM_B)
        yn = d * lax.rsqrt(var + GN_EPS) * lng_ref[...] + lnb_ref[...]
        bonus = _head_sums(row["r"] * row["khf"] * rk_ref[...]) * row["v"]
        ob_ref[bi] = (yn + bonus) * row["gate"]

    @pl.when(c == pl.num_programs(1) - 1)
    def _():
        sout_ref[...] = state_ref[...]


def _rwkv(pb3, shift_prev, wkv_prev, p, chunk, nb, t_valid):
    b, t, _ = pb3.shape
    const = lambda i, j: (0, 0)
    vec = lambda n: pl.BlockSpec((1, n), const)
    state_spec = pl.BlockSpec((nb, N_HEADS_B, HEAD_DIM_B, HEAD_DIM_B), lambda i, j: (i, 0, 0, 0))
    return pl.pallas_call(
        functools.partial(_rwkv_kernel, chunk=chunk, t_valid=t_valid),
        grid=(b // nb, t // chunk),
        in_specs=[
            pl.BlockSpec((nb, chunk, RWKV_PROJ), lambda i, j: (i, j, 0)),
            pl.BlockSpec((nb, 1, RWKV_PROJ), lambda i, j: (i, 0, 0)),
            state_spec,
            vec(RWKV_PROJ), vec(W_B), pl.BlockSpec((D_DECAY_LORA, W_B), const),
            vec(W_B), pl.BlockSpec((D_AAA_LORA, W_B), const), pl.BlockSpec((D_GATE_LORA, W_B), const),
            vec(W_B), vec(W_B), vec(W_B), vec(W_B), vec(W_B),
        ],
        out_specs=[pl.BlockSpec((nb, chunk, W_B), lambda i, j: (i, j, 0)), state_spec],
        out_shape=[jax.ShapeDtypeStruct((b, t, W_B), F32),
                   jax.ShapeDtypeStruct(wkv_prev.shape, F32)],
        scratch_shapes=[pltpu.VMEM((nb, 1, RWKV_PROJ), F32),
                        pltpu.VMEM((nb, N_HEADS_B, HEAD_DIM_B, HEAD_DIM_B), F32)],
        compiler_params=pltpu.CompilerParams(dimension_semantics=("arbitrary", "arbitrary")),
        name="rwkv",
    )(pb3, shift_prev.reshape(b, 1, RWKV_PROJ), wkv_prev,
      p["mu"].reshape(1, -1), p["w0"].reshape(1, -1), p["w_up"], p["a0"].reshape(1, -1), p["a_up"],
      p["g_up"], p["k_k"].reshape(1, -1), p["k_a"].reshape(1, -1), p["r_k"].reshape(1, -1),
      p["ln_g"].reshape(1, -1), p["ln_b"].reshape(1, -1))


def _f32_to_key(s):
    b = lax.bitcast_convert_type(s, jnp.int32)
    return b ^ (lax.shift_right_arithmetic(b, 31) & 0x7FFFFFFF)


def _key_to_f32(k):
    return lax.bitcast_convert_type(k ^ (lax.shift_right_arithmetic(k, 31) & 0x7FFFFFFF), F32)


def _next_key(k):
    n = k + 1
    n = jnp.where((n >= -(1 << 23)) & (n < 0), 0, n)
    return jnp.where((n > 0) & (n < (1 << 23)), 1 << 23, n)


def _confirm_kth(count_ge, guess, live, topk):
    def n_true(m):
        return jnp.sum((m & live).astype(jnp.int32))

    def widen(c):
        lo, hi, step, _ = c
        low = count_ge(_key_to_f32(lo)) < topk
        high = count_ge(_key_to_f32(hi)) >= topk
        return (jnp.where(low, lo - step, lo), jnp.where(high, hi + step, hi), step * 2,
                n_true(low | high))

    def halve(c):
        lo, hi, _ = c
        mid = lo + lax.shift_right_arithmetic(hi - lo, 1)
        ok = count_ge(_key_to_f32(mid)) >= topk
        is_open = hi > _next_key(lo)
        lo, hi = jnp.where(is_open & ok, mid, lo), jnp.where(is_open & ~ok, mid, hi)
        return lo, hi, n_true(hi > _next_key(lo))

    lo, hi, _, _ = lax.while_loop(lambda c: c[3] > 0, widen,
                                  (guess, _next_key(guess), jnp.int32(1 << 12), jnp.int32(1)))
    lo, _, _ = lax.while_loop(lambda c: c[2] > 0, halve, (lo, hi, n_true(hi > _next_key(lo))))
    return lo


def _tie_cutoff(count_eq_before, shape, need, n_bits):
    def body(it, x):
        cand = x + lax.shift_left(jnp.int32(1), n_bits - 1 - it)
        return jnp.where(count_eq_before(cand) < need, cand, x)
    return lax.fori_loop(0, n_bits, body, jnp.zeros(shape, jnp.int32))


def _dsa_prompt_kernel(qa_ref, iq_ref, ikwq_ref, ka_ref, va_ref, ikw_ref, o_ref,
                       sc_ref, kh_ref, kl_ref, cnt_ref, cnt16_ref, tied_ref, m_ref, l_ref, acc_ref, *, tq, tk, topk):
    i = pl.program_id(1)
    n_full = ((i + 1) * tq) // tk
    has_part = ((i + 1) * tq) % tk != 0
    iq = iq_ref[0]
    iq_heads = jnp.concatenate(
        [iq[:, h * IDX_DIM:(h + 1) * IDX_DIM] for h in range(N_IDX_HEADS)], axis=0).astype(BF16)
    w_t = ikwq_ref[0].T
    sizes = sorted({tk, tq})
    kpos0 = {n: lax.broadcasted_iota(jnp.int32, (n, tq), 0) for n in sizes}
    qpos = {n: i * tq + lax.broadcasted_iota(jnp.int32, (n, tq), 1) for n in sizes}
    dn_t = (((1,), (1,)), ((), ()))
    dn_0 = (((0,), (0,)), ((), ()))

    def for_tiles(body, init, carry_ref=None):
        def full(kt, c):
            return body(pl.ds(pl.multiple_of(kt * tk, tk), tk), tk, c)
        c = lax.fori_loop(0, n_full, full, init)
        if tk == tq:
            return c
        last = pl.ds(pl.multiple_of(n_full * tk, tq), tq)
        if carry_ref is None:
            @pl.when(has_part)
            def _():
                body(last, tq, c)
            return c
        carry_ref[...] = c

        @pl.when(has_part)
        def _():
            carry_ref[...] = body(last, tq, carry_ref[...])
        return carry_ref[...]

    def scores(rows, n, carry):
        ik = ikw_ref[0, rows, :][:, :IDX_DIM].astype(BF16)
        d = lax.dot_general(ik, iq_heads, dn_t, preferred_element_type=F32)
        s = jnp.zeros((n, tq), F32)
        for h in range(N_IDX_HEADS):
            s = s + w_t[IDX_DIM + h:IDX_DIM + h + 1, :] * jnp.maximum(d[:, h * tq:(h + 1) * tq], 0.0)
        sc = jnp.where(kpos0[n] + rows.start <= qpos[n], jnp.where(s == 0.0, 0.0, s), -jnp.inf)
        sc_ref[rows, :] = sc
        key = _f32_to_key(sc)
        kh_ref[rows, :] = lax.shift_right_arithmetic(key, 16).astype(jnp.int16)
        kl_ref[rows, :] = ((key & 0xFFFF) + I16_MIN).astype(jnp.int16)
        return carry

    for_tiles(scores, 0)

    def count(pred):
        def body(rows, n, acc):
            m = pred(sc_ref[rows, :]).astype(jnp.int32)
            return acc + jnp.sum(m.reshape(n // SUBLANES, SUBLANES, tq), axis=0)
        acc = for_tiles(body, jnp.zeros((SUBLANES, tq), jnp.int32), cnt_ref)
        return jnp.sum(acc, axis=0, keepdims=True)

    rows16 = 2 * SUBLANES

    def count16(ref, pred):
        def body(rows, n, acc):
            one = jnp.where(pred(ref[rows, :]), jnp.int16(1), jnp.int16(0))
            part = one[0:rows16, :]
            for j in range(rows16, n, rows16):
                part = part + one[j:j + rows16, :]
            return acc + part.astype(jnp.int32)
        acc = for_tiles(body, jnp.zeros((rows16, tq), jnp.int32), cnt16_ref)
        return jnp.sum(acc, axis=0, keepdims=True)

    def kth16(ref, above):
        def body(it, t):
            cand = t + lax.shift_left(jnp.int32(1), 15 - it)
            c16 = cand.astype(jnp.int16)
            return jnp.where(above + count16(ref, lambda blk: blk >= c16) >= topk, cand, t)
        return lax.fori_loop(0, 16, body, jnp.full((1, tq), I16_MIN, jnp.int32))

    t_hi = kth16(kh_ref, 0)
    h16 = t_hi.astype(jnp.int16)
    n_hi = count16(kh_ref, lambda blk: blk > h16)

    def keep_equal(rows, n, carry):
        kl_ref[rows, :] = jnp.where(kh_ref[rows, :] == h16, kl_ref[rows, :], jnp.int16(I16_MIN))
        return carry

    for_tiles(keep_equal, 0)
    guess = lax.shift_left(t_hi, 16) + (kth16(kl_ref, n_hi) - I16_MIN)
    live = i * tq + lax.broadcasted_iota(jnp.int32, (1, tq), 1) + 1 >= topk
    k_thr = _confirm_kth(lambda c: count(lambda blk: blk >= c), guess, live, topk)
    thr = jnp.where(live, _key_to_f32(k_thr), -jnp.inf)
    need = (topk - count(lambda blk: blk > thr)).astype(F32)
    before = {n: (lax.broadcasted_iota(jnp.int32, (n, n), 0)
                  > lax.broadcasted_iota(jnp.int32, (n, n), 1)).astype(BF16) for n in sizes}
    ones = {n: jnp.ones((n, SUBLANES), BF16) for n in sizes}

    def select(rows, n, tied_before):
        blk = sc_ref[rows, :]
        tied = (blk == thr) & live
        tied_b = jnp.where(tied, 1.0, 0.0).astype(BF16)
        rank = tied_before + jnp.dot(before[n], tied_b, preferred_element_type=F32)
        sel = (blk > thr) | (tied & (rank < need))
        sc_ref[rows, :] = jnp.where(sel, 0.0, -jnp.inf)
        return tied_before + lax.dot_general(ones[n], tied_b, dn_0, preferred_element_type=F32)[:1, :]

    for_tiles(select, jnp.zeros((1, tq), F32), tied_ref)

    m_ref[...] = jnp.full(m_ref.shape, -jnp.inf, F32)
    l_ref[...] = jnp.zeros(l_ref.shape, F32)
    acc_ref[...] = jnp.zeros(acc_ref.shape, F32)
    qa = (qa_ref[0] * (HEAD_DIM_A ** -0.5 * math.log2(math.e))).astype(BF16)
    heads = range(N_HEADS_A)
    low_lanes = {n: lax.broadcasted_iota(jnp.int32, (n, W_KV), 1) < HEAD_DIM_A for n in sizes}
    hs = [slice(h * HEAD_DIM_A, (h + 1) * HEAD_DIM_A) for h in heads]
    gs = [slice((h // (N_HEADS_A // N_KV_A)) * HEAD_DIM_A, (h // (N_HEADS_A // N_KV_A) + 1) * HEAD_DIM_A)
          for h in heads]

    def attend(rows, n, carry):
        bias = sc_ref[rows, :]
        kk = ka_ref[0, rows, :].astype(BF16)
        v32 = va_ref[0, rows, :]
        v_aug = [jnp.where(low_lanes[n], v32 if g == 0 else pltpu.roll(v32, HEAD_DIM_A, axis=1), 1.0)
                 .astype(BF16) for g in range(N_KV_A)]
        s = [lax.dot_general(kk[:, gs[h]], qa[:, hs[h]], dn_t, preferred_element_type=F32) + bias
             for h in heads]
        for h in heads:
            m_old = m_ref[h:h + 1, :]
            m_new = jnp.maximum(m_old, jnp.max(s[h], axis=0, keepdims=True))
            m_safe = jnp.where(m_new == -jnp.inf, 0.0, m_new)
            alpha = jnp.exp2(m_old - m_safe)
            p = jnp.exp2((s[h] - m_safe).astype(BF16))
            m_ref[h:h + 1, :] = m_new
            pv = lax.dot_general(v_aug[h // (N_HEADS_A // N_KV_A)], p, dn_0, preferred_element_type=F32)
            l_ref[h:h + 1, :] = alpha * l_ref[h:h + 1, :] + pv[HEAD_DIM_A:HEAD_DIM_A + 1, :]
            acc_ref[hs[h], :] = alpha * acc_ref[hs[h], :] + pv[:HEAD_DIM_A, :]
        return carry

    for_tiles(attend, 0)
    for h in heads:
        acc_ref[hs[h], :] = acc_ref[hs[h], :] / l_ref[h:h + 1, :]
    o_ref[0] = acc_ref[...].T


def _dsa_prompt(qa3, iq3, ikw3, ka3, va3, tq):
    b, t, _ = qa3.shape
    topk = min(TOPK_MAX, t // 4)
    assert tq >= topk and t % tq == 0
    tk = 2 * tq if t % (2 * tq) == 0 else tq
    qtile = lambda w: pl.BlockSpec((1, tq, w), lambda i, j: (i, j, 0))
    whole = lambda w: pl.BlockSpec((1, t, w), lambda i, j: (i, 0, 0))
    return pl.pallas_call(
        functools.partial(_dsa_prompt_kernel, tq=tq, tk=tk, topk=topk),
        grid=(b, t // tq),
        in_specs=[qtile(W_A), qtile(W_IQ), qtile(LANES), whole(W_KV), whole(W_KV), whole(LANES)],
        out_specs=qtile(W_A),
        out_shape=jax.ShapeDtypeStruct((b, t, W_A), F32),
        scratch_shapes=[pltpu.VMEM((t, tq), F32),
                        pltpu.VMEM((t, tq), jnp.int16), pltpu.VMEM((t, tq), jnp.int16),
                        pltpu.VMEM((SUBLANES, tq), jnp.int32), pltpu.VMEM((2 * SUBLANES, tq), jnp.int32),
                        pltpu.VMEM((1, tq), F32),
                        pltpu.VMEM((N_HEADS_A, tq), F32), pltpu.VMEM((N_HEADS_A, tq), F32),
                        pltpu.VMEM((W_A, tq), F32)],
        compiler_params=pltpu.CompilerParams(
            dimension_semantics=("arbitrary", "arbitrary"), vmem_limit_bytes=VMEM_LIMIT_BYTES),
        name="dsa_prompt",
    )(qa3, iq3, ikw3, ka3, va3, ikw3)


def _dsa_sample_kernel(pt_ref, qa_ref, iq_ref, ikwn_ref, kan_ref, van_ref, cik_ref, ck_ref, cv_ref,
                       o_ref, ik_buf, k_buf, v_buf, sems, *, n_pages, page, tn, t_new, topk, n_bits):
    step = pl.program_id(0)
    n_steps = pl.num_programs(0)
    nr = qa_ref.shape[0]
    past = n_pages * page
    total = past + LANES
    slot = step % 2
    sem_k, sem_v = 2, 3

    def ik_copy(row, dst, r, j):
        return pltpu.make_async_copy(cik_ref.at[pt_ref[row, j]],
                                     ik_buf.at[dst, r, :, pl.ds(j * page, page)], sems.at[dst])

    def kv_copies(row, r, j):
        pg, cols = pt_ref[row, j], pl.ds(j * page, page)
        return (pltpu.make_async_copy(ck_ref.at[pg], k_buf.at[r, :, cols], sems.at[sem_k]),
                pltpu.make_async_copy(cv_ref.at[pg], v_buf.at[r, :, cols], sems.at[sem_v]))

    def start_ik(at_step, dst):
        for r in range(nr):
            for j in range(n_pages):
                ik_copy(at_step * nr + r, dst, r, j).start()

    @pl.when(step == 0)
    def _():
        start_ik(0, 0)

    for r in range(nr):
        for j in range(n_pages):
            for cp in kv_copies(step * nr + r, r, j):
                cp.start()

    @pl.when(step + 1 < n_steps)
    def _():
        start_ik(step + 1, 1 - slot)

    for r in range(nr):
        for j in range(n_pages):
            ik_copy(step * nr + r, slot, r, j).wait()

    def tail_tile(x):
        return jnp.concatenate([x, jnp.zeros((LANES - tn, x.shape[1]), F32)], axis=0).T

    rows_all = nr * tn
    keys = []
    for r in range(nr):
        ikw = ikwn_ref[r]
        ik_buf[slot, r, :, past:] = tail_tile(ikw)[:IDX_DIM, :]
        iq = iq_ref[r]
        iq_heads = jnp.concatenate(
            [iq[:, h * IDX_DIM:(h + 1) * IDX_DIM] for h in range(N_IDX_HEADS)], axis=0).astype(BF16)
        d = jnp.dot(iq_heads, ik_buf[slot, r].astype(BF16), preferred_element_type=F32)
        s = jnp.zeros((tn, total), F32)
        for h in range(N_IDX_HEADS):
            s = s + ikw[:, IDX_DIM + h:IDX_DIM + h + 1] * jnp.maximum(d[h * tn:(h + 1) * tn, :], 0.0)
        keys.append(jnp.where(s == 0.0, 0.0, s))
    kpos = lax.broadcasted_iota(jnp.int32, (rows_all, total), 1)
    tok = lax.broadcasted_iota(jnp.int32, (rows_all, total), 0) % tn
    key = jnp.where(kpos <= past + tok, jnp.concatenate(keys, axis=0), -jnp.inf)

    def count(m):
        return jnp.sum(m.astype(jnp.int32), axis=1, keepdims=True)

    def count16(m):
        one = jnp.where(m, jnp.int16(1), jnp.int16(0))
        part = one[:, :LANES]
        for j in range(LANES, total, LANES):
            part = part + one[:, j:j + LANES]
        return jnp.sum(part.astype(jnp.int32), axis=1, keepdims=True)

    def kth16(arr, above):
        def body(it, t):
            w = lax.shift_left(jnp.int32(1), 14 - 2 * it)
            hits = [(above + count16(arr >= (t + k * w).astype(jnp.int16)) >= topk).astype(jnp.int32)
                    for k in (1, 2, 3)]
            return t + (hits[0] + hits[1] + hits[2]) * w
        return lax.fori_loop(0, 8, body, jnp.full((rows_all, 1), I16_MIN, jnp.int32))

    ikey = _f32_to_key(key)
    kh = lax.shift_right_arithmetic(ikey, 16).astype(jnp.int16)
    t_hi = kth16(kh, 0)
    h16 = t_hi.astype(jnp.int16)
    kl = jnp.where(kh == h16, ((ikey & 0xFFFF) + I16_MIN).astype(jnp.int16), jnp.int16(I16_MIN))
    guess = lax.shift_left(t_hi, 16) + (kth16(kl, count16(kh > h16)) - I16_MIN)
    live = past + lax.broadcasted_iota(jnp.int32, (rows_all, 1), 0) % tn + 1 >= topk
    thr = jnp.where(live, _key_to_f32(_confirm_kth(lambda c: count(key >= c), guess, live, topk)), -jnp.inf)
    n_gt = count(key > thr)
    tied = (key == thr) & live
    real = tok[:, :LANES] < t_new
    surplus = jnp.sum((real & (n_gt + count(tied) > topk)).astype(jnp.int32))
    cut = lax.cond(
        surplus > 0,
        lambda: _tie_cutoff(lambda x: count(tied & (kpos < x)), (rows_all, 1), topk - n_gt, n_bits),
        lambda: jnp.full((rows_all, 1), 2 ** 31 - 1, jnp.int32))
    bias = jnp.where((key > thr) | (tied & (kpos <= cut)), 0.0, -jnp.inf)

    for r in range(nr):
        for j in range(n_pages):
            for cp in kv_copies(step * nr + r, r, j):
                cp.wait()

    rep = N_HEADS_A // N_KV_A
    for r in range(nr):
        k_buf[r, :, past:] = tail_tile(kan_ref[r])
        v_buf[r, :, past:] = tail_tile(van_ref[r])
        qa = (qa_ref[r] * (HEAD_DIM_A ** -0.5)).astype(BF16)
        bias_g = jnp.concatenate([bias[r * tn:(r + 1) * tn]] * rep, axis=0)
        for g in range(N_KV_A):
            gs = slice(g * HEAD_DIM_A, (g + 1) * HEAD_DIM_A)
            qg = jnp.concatenate(
                [qa[:, (g * rep + i) * HEAD_DIM_A:(g * rep + i + 1) * HEAD_DIM_A] for i in range(rep)], axis=0)
            sc = jnp.dot(qg, k_buf[r, gs, :].astype(BF16), preferred_element_type=F32) + bias_g
            m = jnp.max(sc, axis=1, keepdims=True)
            p = jnp.exp(sc - m)
            og = lax.dot_general(p.astype(BF16), v_buf[r, gs, :].astype(BF16), (((1,), (1,)), ((), ())),
                                 preferred_element_type=F32) / jnp.sum(p, axis=1, keepdims=True)
            for i in range(rep):
                h = g * rep + i
                o_ref[r, :, h * HEAD_DIM_A:(h + 1) * HEAD_DIM_A] = og[i * tn:(i + 1) * tn, :]


def _dsa_sample(qa3, iq3, ikw3, ka3, va3, cache_idx_k, cache_k, cache_v, page_table, t_new):
    db, tn, _ = qa3.shape
    n_pages = page_table.shape[1]
    n_phys, page, _ = cache_idx_k.shape
    past = n_pages * page
    total = past + LANES
    topk = min(TOPK_MAX, (past + t_new) // 4)
    n_bits = max(1, int(math.ceil(math.log2(total))))
    nr = _pick_tile(db, 2)
    cik_t = jnp.transpose(cache_idx_k, (0, 2, 1))
    ck_t = jnp.transpose(cache_k, (0, 2, 3, 1)).reshape(n_phys, W_KV, page)
    cv_t = jnp.transpose(cache_v, (0, 2, 3, 1)).reshape(n_phys, W_KV, page)
    new = lambda w: pl.BlockSpec((nr, tn, w), lambda i, pt: (i, 0, 0))
    hbm = pl.BlockSpec(memory_space=pl.ANY)
    return pl.pallas_call(
        functools.partial(_dsa_sample_kernel, n_pages=n_pages, page=page, tn=tn, t_new=t_new, topk=topk,
                          n_bits=n_bits),
        grid_spec=pltpu.PrefetchScalarGridSpec(
            num_scalar_prefetch=1,
            grid=(db // nr,),
            in_specs=[new(W_A), new(W_IQ), new(LANES), new(W_KV), new(W_KV), hbm, hbm, hbm],
            out_specs=new(W_A),
            scratch_shapes=[pltpu.VMEM((2, nr, IDX_DIM, total), F32), pltpu.VMEM((nr, W_KV, total), F32),
                            pltpu.VMEM((nr, W_KV, total), F32), pltpu.SemaphoreType.DMA((4,))],
        ),
        out_shape=jax.ShapeDtypeStruct((db, tn, W_A), F32),
        compiler_params=pltpu.CompilerParams(
            dimension_semantics=("arbitrary",), vmem_limit_bytes=VMEM_LIMIT_BYTES),
        name="dsa_sample",
    )(page_table, qa3, iq3, ikw3, ka3, va3, cik_t, ck_t, cv_t)


def _pack_w_in(w_in):
    split = SEG_IKW[0] + IDX_DIM + N_IDX_HEADS
    pad = jnp.zeros((D_MODEL, _W_IK_PAD), w_in.dtype)
    return jnp.concatenate([w_in[:, :split], pad, w_in[:, split:]], axis=1).astype(BF16)


def _pick_tile(n, pref):
    t = min(n, pref)
    while n % t:
        t //= 2
    return t


def _group(x, pos, dsa_fn, mem_k3, mem_v3, shift_prev, wkv_prev, t_valid, wts):
    b, t, _ = x.shape
    n = b * t
    x2 = x.reshape(n, D_MODEL)
    tm = _pick_tile(n, 256)
    qa, ka, va, iq, ikw, pb, cq, gs = _inproj(x2, pos, wts["norm_mix_g"], wts["w_in"], wts["b_gate"], tm)
    r3 = lambda a: a.reshape(b, t, a.shape[-1])
    oa = dsa_fn(r3(qa), r3(iq), r3(ikw), r3(ka), r3(va))
    chunk = _pick_tile(t, RWKV_CHUNK)
    nb = _pick_tile(b, max(2, 32 // chunk))
    ob, wkv_new = _rwkv(r3(pb), shift_prev, wkv_prev, wts["rwkv"], chunk, nb, t_valid)
    tq = _pick_tile(t, 512)
    oc = _xattn(r3(cq), mem_k3, mem_v3, tq, _pick_tile(b, max(1, 64 // tq)))
    y = _merge_mlp(x2, oa.reshape(n, W_A), ob.reshape(n, W_B), oc.reshape(n, W_C), gs,
                   wts["w_branch_a"], wts["w_branch_b"], wts["w_branch_c"], wts["w_out"],
                   wts["norm_mlp_g"], wts["w_mlp_up"], wts["w_mlp_down"], wts["norm_final_g"], tm)
    return y.reshape(b, t, D_MODEL), r3(ka), r3(va), r3(ikw)[..., :IDX_DIM], wkv_new, r3(pb)


def kernel(x_prompt, x_sample, mem_prompt, cache_k, cache_v, cache_idx_k, cache_mem_k, cache_mem_v,
           state_wkv, state_shift, page_table, norm_mix_g, norm_mem_g, norm_mlp_g, norm_final_g,
           w_in, b_gate, w_mem_kv, rwkv_mu, rwkv_w0, rwkv_w_up, rwkv_a0, rwkv_a_up, rwkv_g_up,
           rwkv_k_k, rwkv_k_a, rwkv_r_k, rwkv_ln_g, rwkv_ln_b, w_branch_a, w_branch_b, w_branch_c,
           w_out, w_mlp_up, w_mlp_down):
    bf = lambda w: w.astype(BF16)
    wts = dict(
        norm_mix_g=norm_mix_g, norm_mlp_g=norm_mlp_g, norm_final_g=norm_final_g,
        w_in=_pack_w_in(w_in), b_gate=b_gate,
        rwkv=dict(mu=rwkv_mu, w0=rwkv_w0, w_up=rwkv_w_up, a0=rwkv_a0, a_up=rwkv_a_up, g_up=rwkv_g_up,
                  k_k=rwkv_k_k, k_a=rwkv_k_a, r_k=rwkv_r_k, ln_g=rwkv_ln_g, ln_b=rwkv_ln_b),
        w_branch_a=bf(w_branch_a), w_branch_b=bf(w_branch_b), w_branch_c=bf(w_branch_c),
        w_out=bf(w_out), w_mlp_up=bf(w_mlp_up), w_mlp_down=bf(w_mlp_down))

    b, t, _ = x_prompt.shape
    n_mem = mem_prompt.shape[1]
    mem2 = mem_prompt.reshape(b * n_mem, D_MODEL)
    mk, mv = _memkv(mem2, norm_mem_g, bf(w_mem_kv), _pick_tile(b * n_mem, 256))
    mk3, mv3 = (a.reshape(b, n_mem * N_HEADS_C, HEAD_DIM_C) for a in (mk, mv))
    shift0 = jnp.zeros((b, RWKV_PROJ), F32)
    wkv0 = jnp.zeros((b, N_HEADS_B, HEAD_DIM_B, HEAD_DIM_B), F32)
    dsa_p = functools.partial(_dsa_prompt, tq=_pick_tile(t, 256))
    y_p, k_p, v_p, ik_p, wkv_p, pb_p = _group(
        x_prompt, jnp.arange(t), dsa_p, mk3, mv3, shift0, wkv0, None, wts)

    db, tn, _ = x_sample.shape
    n_pages = page_table.shape[1]
    page = cache_idx_k.shape[1]
    past = n_pages * page
    tp = -(-tn // SUBLANES) * SUBLANES
    x_s = jnp.pad(x_sample, ((0, 0), (0, tp - tn), (0, 0)))
    dsa_s = functools.partial(_dsa_sample, cache_idx_k=cache_idx_k, cache_k=cache_k, cache_v=cache_v,
                              page_table=page_table, t_new=tn)
    y_s, k_s, v_s, ik_s, wkv_s, pb_s = _group(
        x_s, past + jnp.arange(tp), dsa_s, cache_mem_k.reshape(db, n_mem * N_HEADS_C, HEAD_DIM_C),
        cache_mem_v.reshape(db, n_mem * N_HEADS_C, HEAD_DIM_C), state_shift, state_wkv, tn, wts)

    heads = lambda a: a.reshape(a.shape[0], a.shape[1], N_KV_A, HEAD_DIM_A)
    memh = lambda a: a.reshape(b, n_mem, N_HEADS_C, HEAD_DIM_C)
    return (y_p, y_s[:, :tn], heads(k_p), heads(v_p), ik_p, memh(mk3), memh(mv3), wkv_p, pb_p[:, -1],
            heads(k_s[:, :tn]), heads(v_s[:, :tn]), ik_s[:, :tn], wkv_s, pb_s[:, tn - 1])
```
